```python
import math
import jax, jax.numpy as jnp
from jax import lax
import numpy as np

D_MODEL = 1024
BATCH = 8
SEQ = 4096
DEPTH = 1

MIX_WIDTH = D_MODEL
HEAD_DIM = 64
N_HEADS = (MIX_WIDTH // 2) // HEAD_DIM
ATTN_WIDTH = N_HEADS * HEAD_DIM
SSM_GROUP_CH = 16
SSM_STATE = 64
SSM_WIDTH = MIX_WIDTH - ATTN_WIDTH
SSM_GROUPS = SSM_WIDTH // SSM_GROUP_CH
DILATED_BRANCHES = ((128, 1), (512, 4), (2048, 16))
PLE_DIM = 256
N_EXPERTS = 64
TOP_K = 8
N_EXPERT_GROUPS = 8
TOPK_GROUPS = 4
EXPERT_HIDDEN = 256
SHARED_HIDDEN = 256
ROUTED_SCALE = 2.5
BLOCK_ROWS = 128
LN_EPS = 1e-5

kernel_name = 'hybrid_dilated_attn_s5_moe_deepnorm'


def layer_norm(x, g, b):
    xf = x.astype(jnp.float32)
    mu = xf.mean(-1, keepdims=True)
    var = jnp.square(xf - mu).mean(-1, keepdims=True)
    y = (xf - mu) * lax.rsqrt(var + LN_EPS) * g.astype(jnp.float32) + b.astype(jnp.float32)
    return y.astype(x.dtype)


def alibi_slopes(n_heads):
    return jnp.exp2(-8.0 * jnp.arange(1, n_heads + 1, dtype=jnp.float32) / n_heads)


def dilated_branch(q, k, v, dil, n_keys, slopes):
    b, s, h, e = q.shape
    span = dil * n_keys
    sp = -(-s // span) * span
    L = sp // dil
    nb = L // n_keys
    pad = ((0, 0), (0, sp - s), (0, 0), (0, 0))

    def split(t):
        t = jnp.pad(t, pad).reshape(b, L, dil, h, e).transpose(0, 2, 1, 3, 4)
        return t.reshape(b, dil, nb, n_keys, h, e)

    def with_prev(t):
        prev = jnp.pad(t, ((0, 0), (0, 0), (1, 0), (0, 0), (0, 0), (0, 0)))[:, :, :-1]
        return jnp.concatenate([prev, t], axis=3)

    qb = split(q)
    kw = with_prev(split(k))
    vw = with_prev(split(v))
    scores = jnp.einsum('brnqhe,brnkhe->brnhqk', qb, kw,
                        preferred_element_type=jnp.float32) * (HEAD_DIM ** -0.5)
    qi = jnp.arange(n_keys)[:, None]
    kj = jnp.arange(2 * n_keys)[None, :]
    steps = qi + n_keys - kj
    in_band = (steps >= 0) & (steps <= n_keys)
    first_blk = (jnp.arange(nb) == 0)[:, None, None]
    valid = in_band[None] & ~(first_blk & (kj < n_keys)[None])
    bias = -slopes[:, None, None] * (steps * dil).astype(jnp.float32)[None]
    scores = jnp.where(valid[None, None, :, None], scores + bias, -jnp.inf)
    m = scores.max(-1, keepdims=True)
    pexp = jnp.exp(scores - m)
    l = pexp.sum(-1)
    o = jnp.einsum('brnhqk,brnkhe->brnqhe', pexp, vw.astype(jnp.float32))
    o = o / jnp.swapaxes(l, 3, 4)[..., None]
    lse = jnp.swapaxes(m[..., 0] + jnp.log(l), 3, 4)
    o = o.reshape(b, dil, L, h, e).transpose(0, 2, 1, 3, 4).reshape(b, sp, h, e)[:, :s]
    lse = lse.reshape(b, dil, L, h).transpose(0, 2, 1, 3).reshape(b, sp, h)[:, :s]
    return o, lse


def dilated_attention(q, k, v):
    slopes = alibi_slopes(q.shape[2])
    outs, lses = [], []
    for window, dil in DILATED_BRANCHES:
        o, lse = dilated_branch(q, k, v, dil, window // dil, slopes)
        outs.append(o)
        lses.append(lse)
    w = jax.nn.softmax(jnp.stack(lses, 0), axis=0)
    return jnp.sum(w[..., None] * jnp.stack(outs, 0), axis=0)


def s5_mixer(u, lam_re, lam_im, log_dt, b_re, b_im, c_re, c_im, d_skip, w_glu, b_glu):
    b, s, w = u.shape
    f32 = jnp.float32
    uf = u.astype(f32).reshape(b, s, SSM_GROUPS, SSM_GROUP_CH)
    lam = lax.complex(lam_re.astype(f32), lam_im.astype(f32))
    dt = jnp.exp(log_dt.astype(f32))[:, None]
    lam_bar = jnp.exp(lam * dt)
    gain = (lam_bar - 1.0) / lam
    b_bar = gain[..., None] * lax.complex(b_re.astype(f32), b_im.astype(f32))
    bu = jnp.einsum('bsgc,gnc->bsgn', uf.astype(jnp.complex64), b_bar)
    a = jnp.broadcast_to(lam_bar, bu.shape)

    def combine(left, right):
        a1, x1 = left
        a2, x2 = right
        return a1 * a2, a2 * x1 + x2

    _, states = lax.associative_scan(combine, (a, bu), axis=1)
    y = (jnp.einsum('gcn,bsgn->bsgc', c_re.astype(f32), states.real)
         - jnp.einsum('gcn,bsgn->bsgc', c_im.astype(f32), states.imag))
    y = (y + d_skip.astype(f32) * uf).reshape(b, s, w)
    y = jax.nn.gelu(y)
    y = y * jax.nn.sigmoid(y @ w_glu.astype(f32) + b_glu.astype(f32))
    return y.astype(u.dtype)


def routed_moe(h, w_router, router_bias, w_gate, w_up, w_down):
    t, d = h.shape
    scores = jax.nn.sigmoid((h @ w_router).astype(jnp.float32))
    sel = scores + router_bias.astype(jnp.float32)
    per_group = N_EXPERTS // N_EXPERT_GROUPS
    grp_score = lax.top_k(sel.reshape(t, N_EXPERT_GROUPS, per_group), 2)[0].sum(-1)
    _, top_g = lax.top_k(grp_score, TOPK_GROUPS)
    gmask = jax.nn.one_hot(top_g, N_EXPERT_GROUPS).sum(1) > 0
    emask = jnp.repeat(gmask, per_group, axis=1)
    _, idx = lax.top_k(jnp.where(emask, sel, -jnp.inf), TOP_K)
    wts = jnp.take_along_axis(scores, idx, axis=1)
    wts = wts / wts.sum(-1, keepdims=True) * ROUTED_SCALE

    n_assign = t * TOP_K
    flat_e = idx.reshape(n_assign)
    flat_w = wts.reshape(n_assign)
    flat_tok = jnp.arange(n_assign, dtype=jnp.int32) // TOP_K
    order = jnp.argsort(flat_e)
    se = flat_e[order]
    counts = jnp.bincount(flat_e, length=N_EXPERTS)
    start = jnp.cumsum(counts) - counts
    pcounts = (counts + BLOCK_ROWS - 1) // BLOCK_ROWS * BLOCK_ROWS
    pend = jnp.cumsum(pcounts)
    pstart = pend - pcounts
    dest = pstart[se] + (jnp.arange(n_assign) - start[se])
    n_blocks = -(-n_assign // BLOCK_ROWS) + N_EXPERTS
    n_rows = n_blocks * BLOCK_ROWS
    buf_tok = jnp.full((n_rows,), t, jnp.int32).at[dest].set(flat_tok[order])
    buf_w = jnp.zeros((n_rows,), jnp.float32).at[dest].set(flat_w[order])
    block_e = jnp.minimum(jnp.searchsorted(pend, jnp.arange(n_blocks) * BLOCK_ROWS, side='right'),
                          N_EXPERTS - 1)
    h_pad = jnp.concatenate([h, jnp.zeros((1, d), h.dtype)], axis=0)

    def run_block(args):
        tok, e, wb = args
        xb = h_pad[tok]
        hid = jax.nn.silu(xb @ w_gate[e]) * (xb @ w_up[e])
        return (hid @ w_down[e]) * wb[:, None].astype(h.dtype)

    yb = lax.map(run_block, (buf_tok.reshape(n_blocks, BLOCK_ROWS), block_e,
                             buf_w.reshape(n_blocks, BLOCK_ROWS)))
    y = jnp.zeros((t + 1, d), h.dtype).at[buf_tok].add(yb.reshape(n_rows, d))
    return y[:t]


def setup_inputs(seed: int = 0) -> dict:
    key = jax.random.key(seed)
    ks = jax.random.split(key, 32)
    f32 = jnp.float32
    beta = (8.0 * DEPTH) ** -0.25
    L, D, G, N, C = DEPTH, D_MODEL, SSM_GROUPS, SSM_STATE, SSM_GROUP_CH
    E, H, HS = N_EXPERTS, EXPERT_HIDDEN, SHARED_HIDDEN
    nrm = lambda k, shp, sc: jax.random.normal(k, shp, f32) * sc
    in_cols = 3 * ATTN_WIDTH + SSM_WIDTH
    lam_im0 = jnp.pi * jnp.arange(N, dtype=f32)
    return {
        'x': nrm(ks[0], (BATCH, SEQ, D), 1.0),
        'p': nrm(ks[1], (DEPTH, BATCH, SEQ, PLE_DIM), 1.0),
        'w_in': nrm(ks[2], (L, D, in_cols), D ** -0.5),
        'lam_re': -0.5 + nrm(ks[3], (L, G, N), 0.01),
        'lam_im': lam_im0 + nrm(ks[4], (L, G, N), 0.01),
        'log_dt': jax.random.uniform(ks[5], (L, G), f32, math.log(1e-3), math.log(1e-1)),
        'b_re': nrm(ks[6], (L, G, N, C), (2.0 * C) ** -0.5),
        'b_im': nrm(ks[7], (L, G, N, C), (2.0 * C) ** -0.5),
        'c_re': nrm(ks[8], (L, G, C, N), (2.0 * N) ** -0.5),
        'c_im': nrm(ks[9], (L, G, C, N), (2.0 * N) ** -0.5),
        'd_skip': nrm(ks[10], (L, G, C), 1.0),
        'w_glu': nrm(ks[11], (L, SSM_WIDTH, SSM_WIDTH), SSM_WIDTH ** -0.5),
        'b_glu': nrm(ks[12], (L, SSM_WIDTH), 0.01),
        'w_out': nrm(ks[13], (L, MIX_WIDTH, D), beta * MIX_WIDTH ** -0.5),
        'ln1_g': 1.0 + nrm(ks[14], (L, D), 0.02),
        'ln1_b': nrm(ks[15], (L, D), 0.02),
        'w_router': nrm(ks[16], (L, D, E), D ** -0.5),
        'router_bias': nrm(ks[17], (L, E), 0.01),
        'w_gate': nrm(ks[18], (L, E, D, H), D ** -0.5),
        'w_up': nrm(ks[19], (L, E, D, H), D ** -0.5),
        'w_down': nrm(ks[20], (L, E, H, D), beta * H ** -0.5),
        'ws_gate': nrm(ks[21], (L, D, HS), D ** -0.5),
        'ws_up': nrm(ks[22], (L, D, HS), D ** -0.5),
        'ws_down': nrm(ks[23], (L, HS, D), beta * HS ** -0.5),
        'w_ple': nrm(ks[24], (L, PLE_DIM, D), beta * PLE_DIM ** -0.5),
        'w_ple_gate': nrm(ks[25], (L, D, D), D ** -0.5),
        'ln2_g': 1.0 + nrm(ks[26], (L, D), 0.02),
        'ln2_b': nrm(ks[27], (L, D), 0.02),
    }


def reference(x, p, w_in, lam_re, lam_im, log_dt, b_re, b_im, c_re, c_im, d_skip,
              w_glu, b_glu, w_out, ln1_g, ln1_b, w_router, router_bias, w_gate, w_up,
              w_down, ws_gate, ws_up, ws_down, w_ple, w_ple_gate, ln2_g, ln2_b):
    alpha = (2.0 * DEPTH) ** 0.25
    b, s, d = x.shape
    h = x
    for i in range(DEPTH):
        z = h @ w_in[i]
        q = z[..., :ATTN_WIDTH].reshape(b, s, N_HEADS, HEAD_DIM)
        k = z[..., ATTN_WIDTH:2 * ATTN_WIDTH].reshape(b, s, N_HEADS, HEAD_DIM)
        v = z[..., 2 * ATTN_WIDTH:3 * ATTN_WIDTH].reshape(b, s, N_HEADS, HEAD_DIM)
        u = z[..., 3 * ATTN_WIDTH:]
        attn = dilated_attention(q, k, v).reshape(b, s, ATTN_WIDTH).astype(h.dtype)
        ssm = s5_mixer(u, lam_re[i], lam_im[i], log_dt[i], b_re[i], b_im[i], c_re[i],
                       c_im[i], d_skip[i], w_glu[i], b_glu[i])
        mix = jnp.concatenate([attn, ssm], axis=-1) @ w_out[i]
        h = layer_norm(alpha * h + mix, ln1_g[i], ln1_b[i])
        hf = h.reshape(b * s, d)
        routed = routed_moe(hf, w_router[i], router_bias[i], w_gate[i], w_up[i], w_down[i])
        shared = (jax.nn.silu(hf @ ws_gate[i]) * (hf @ ws_up[i])) @ ws_down[i]
        ffn = (routed + shared).reshape(b, s, d)
        ple = (p[i] @ w_ple[i]) * jax.nn.sigmoid(h @ w_ple_gate[i])
        h = layer_norm(alpha * h + ffn + ple, ln2_g[i], ln2_b[i])
    return h
```

```python
import functools
import math

import numpy as np
import jax
import jax.numpy as jnp
from jax import lax
from jax.experimental import pallas as pl
from jax.experimental.pallas import tpu as pltpu

F32 = jnp.float32
BF16 = jnp.bfloat16

HEAD_DIM = 64
N_HEADS = 8
ATTN_WIDTH = N_HEADS * HEAD_DIM
HEAD_LANES = 128
SSM_GROUP_CH = 16
SSM_STATE = 64
DILATED_BRANCHES = ((128, 1), (512, 4), (2048, 16))
N_EXPERTS = 64
TOP_K = 8
N_EXPERT_GROUPS = 8
TOPK_GROUPS = 4
ROUTED_SCALE = 2.5
LN_EPS = 1e-5
MASK_VALUE = -1e30

ATT_BLOCK = 256
ATT_LOOKBACK = max(w for w, _ in DILATED_BRANCHES) // ATT_BLOCK
QKV_ROWS = 512
SSM_CHUNK = 128
SSM_PERM_T = 32
SSM_CH_BLOCK = 128
POST_ROWS = 512
MOE_ROWS = 1024
MOE_EXPERTS_PER_STEP = 4
VMEM_LIMIT = 56 * 1024 * 1024


def _params(sem, vmem=VMEM_LIMIT):
    return pltpu.CompilerParams(dimension_semantics=sem, vmem_limit_bytes=vmem)


def _qkv_kernel(x_ref, w_ref, pos_ref, q_ref, k_ref, v_ref):
    xb = x_ref[0].astype(BF16)
    width = q_ref.shape[-1]
    for c, o_ref in enumerate((q_ref, k_ref, v_ref)):
        cols = slice(c * width, (c + 1) * width)
        z = jnp.dot(xb, w_ref[:, cols], preferred_element_type=F32)
        o_ref[0] = (z + pos_ref[:, cols].astype(F32)).astype(BF16)


def _qkv_tables(seq):
    t = np.arange(seq)
    width = N_HEADS * HEAD_LANES
    qpos = np.zeros((seq, width), np.float32)
    kpos = np.zeros((seq, width), np.float32)
    vpos = np.zeros((seq, width), np.float32)
    for h in range(N_HEADS):
        slope = 2.0 ** (-8.0 * (h + 1) / N_HEADS)
        base = h * HEAD_LANES + HEAD_DIM
        qpos[:, base + 0] = -slope * (t % ATT_BLOCK)
        qpos[:, base + 1] = -slope * ATT_BLOCK * (t // ATT_BLOCK)
        qpos[:, base + 2] = slope
        qpos[:, base + 3] = slope * ATT_BLOCK
        kpos[:, base + 0] = 1.0
        kpos[:, base + 1] = 1.0
        kpos[:, base + 2] = t % ATT_BLOCK
        kpos[:, base + 3] = t // ATT_BLOCK
        ones_lane = h * HEAD_LANES + (HEAD_DIM if h % 2 == 0 else 0)
        vpos[:, ones_lane] = 1.0
    return np.concatenate([qpos, kpos, vpos], axis=1)


def _qkv_weights(w_in):
    d = w_in.shape[0]
    pad_hi = ((0, 0), (0, 0), (0, HEAD_LANES - HEAD_DIM))
    pad_lo = ((0, 0), (0, 0), (HEAD_LANES - HEAD_DIM, 0))
    wq = w_in[:, :ATTN_WIDTH].reshape(d, N_HEADS, HEAD_DIM) * (HEAD_DIM ** -0.5)
    wk = w_in[:, ATTN_WIDTH:2 * ATTN_WIDTH].reshape(d, N_HEADS, HEAD_DIM)
    wv = w_in[:, 2 * ATTN_WIDTH:3 * ATTN_WIDTH].reshape(d, N_HEADS // 2, 2, HEAD_DIM)
    wq = jnp.pad(wq, pad_hi).reshape(d, -1)
    wk = jnp.pad(wk, pad_hi).reshape(d, -1)
    wv = jnp.stack([jnp.pad(wv[:, :, 0], pad_hi), jnp.pad(wv[:, :, 1], pad_lo)], axis=2).reshape(d, -1)
    return jnp.concatenate([wq, wk, wv], axis=1).astype(BF16)


def _qkv_call(x, w_qkv, pos):
    b, s, d = x.shape
    width = N_HEADS * HEAD_LANES
    rows = min(QKV_ROWS, s)
    out = jax.ShapeDtypeStruct((b, s, width), BF16)
    blk = pl.BlockSpec((1, rows, width), lambda si, bi: (bi, si, 0))
    return pl.pallas_call(
        _qkv_kernel,
        grid=(s // rows, b),
        in_specs=[pl.BlockSpec((1, rows, d), lambda si, bi: (bi, si, 0)),
                  pl.BlockSpec((d, 3 * width), lambda si, bi: (0, 0)),
                  pl.BlockSpec((rows, 3 * width), lambda si, bi: (si, 0))],
        out_specs=[blk, blk, blk],
        out_shape=[out, out, out],
        compiler_params=_params(("arbitrary", "arbitrary")),
        name="qkv_proj",
    )(x, w_qkv, pos)


def _log_multiplicity_table():
    a = np.arange(ATT_LOOKBACK + 1)[:, None, None]
    i = np.arange(ATT_BLOCK)[None, :, None]
    j = np.arange(ATT_BLOCK)[None, None, :]
    dist = ATT_BLOCK * a + i - j
    mult = np.zeros(dist.shape, np.float64)
    for window, dil in DILATED_BRANCHES:
        mult += (dist >= 0) & (dist <= window) & (dist % dil == 0)
    return np.where(mult > 0, np.log(np.maximum(mult, 1.0)), MASK_VALUE).astype(np.float32)


def _attn_kernel(q_ref, k_ref, v_ref, bias_ref, o_ref):
    qi = pl.program_id(1)
    n_kv = jnp.minimum(qi, ATT_LOOKBACK) + 1
    lane = lax.broadcasted_iota(jnp.int32, (ATT_BLOCK, HEAD_LANES), 1)
    outs = []
    for h in range(N_HEADS):
        lanes = slice(h * HEAD_LANES, (h + 1) * HEAD_LANES)
        q_h = q_ref[0, :, lanes]

        def body(a, carry, lanes=lanes, q_h=q_h):
            m_old, acc = carry
            start = pl.multiple_of((qi - a) * ATT_BLOCK, ATT_BLOCK)
            k_blk = k_ref[0, pl.ds(start, ATT_BLOCK), lanes]
            v_blk = v_ref[0, pl.ds(start, ATT_BLOCK), lanes]
            s = lax.dot_general(q_h, k_blk, (((1,), (1,)), ((), ())), preferred_element_type=F32)
            s = s + bias_ref[a]
            m_new = jnp.maximum(m_old, jnp.max(s, axis=1, keepdims=True))
            p = jnp.exp(s - m_new)
            acc = acc * jnp.exp(m_old - m_new) + jnp.dot(p.astype(BF16), v_blk, preferred_element_type=F32)
            return m_new, acc

        m0 = jnp.full((ATT_BLOCK, 1), MASK_VALUE, F32)
        acc0 = jnp.zeros((ATT_BLOCK, HEAD_LANES), F32)
        _, acc = lax.fori_loop(0, n_kv, body, (m0, acc0))
        ones_lane = HEAD_DIM if h % 2 == 0 else 0
        outs.append(acc * (1.0 / acc[:, ones_lane:ones_lane + 1]))
    for pair in range(N_HEADS // 2):
        o_pair = jnp.where(lane < HEAD_DIM, outs[2 * pair], outs[2 * pair + 1])
        o_ref[0, :, pair * HEAD_LANES:(pair + 1) * HEAD_LANES] = o_pair.astype(BF16)


def _attn_call(q, k, v, bias):
    b, s, width = q.shape
    return pl.pallas_call(
        _attn_kernel,
        grid=(b, s // ATT_BLOCK),
        in_specs=[pl.BlockSpec((1, ATT_BLOCK, width), lambda bi, qi: (bi, qi, 0)),
                  pl.BlockSpec((1, s, width), lambda bi, qi: (bi, 0, 0)),
                  pl.BlockSpec((1, s, width), lambda bi, qi: (bi, 0, 0)),
                  pl.BlockSpec(bias.shape, lambda bi, qi: (0, 0, 0))],
        out_specs=pl.BlockSpec((1, ATT_BLOCK, ATTN_WIDTH), lambda bi, qi: (bi, qi, 0)),
        out_shape=jax.ShapeDtypeStruct((b, s, ATTN_WIDTH), BF16),
        compiler_params=_params(("arbitrary", "arbitrary")),
        name="banded_attention",
    )(q, k, v, bias)


def _gelu_tanh(y):
    return 0.5 * y * (1.0 + jnp.tanh(math.sqrt(2.0 / math.pi) * (y + 0.044715 * (y * y * y))))


def _ssm_kernel(x_ref, wu_ref, perm_ref, perm_t_ref, bmat_ref, cmat_ref, a_re_ref, a_im_ref, dskip_ref,
                wglu_ref, bglu_ref, o_ref, st_re, st_im, u_scr, bu_scr, y_scr):
    n_batch, chunk, _ = x_ref.shape
    n_perm = chunk // SSM_PERM_T
    perm_rows = n_batch * SSM_PERM_T
    n_blocks = bmat_ref.shape[0]
    half = bmat_ref.shape[2] // 2

    @pl.when(pl.program_id(0) == 0)
    def _():
        st_re[...] = jnp.zeros_like(st_re)
        st_im[...] = jnp.zeros_like(st_im)

    for g in range(n_perm):
        xg = x_ref[:, g * SSM_PERM_T:(g + 1) * SSM_PERM_T, :].reshape(perm_rows, x_ref.shape[2])
        ug = jnp.dot(xg.astype(BF16), wu_ref[...], preferred_element_type=F32).astype(BF16)
        u_scr[g * perm_rows:(g + 1) * perm_rows, :] = jnp.dot(
            perm_ref[...], ug, preferred_element_type=F32).astype(BF16)

    for j in range(n_blocks):
        ch = slice(j * SSM_CH_BLOCK, (j + 1) * SSM_CH_BLOCK)
        bu_scr[...] = jnp.dot(u_scr[:, ch], bmat_ref[j], preferred_element_type=F32)
        a_re = jnp.broadcast_to(a_re_ref[j], (n_batch, half))
        a_im = jnp.broadcast_to(a_im_ref[j], (n_batch, half))

        def step(t, carry, a_re=a_re, a_im=a_im):
            s_re, s_im = carry
            rows = pl.ds(pl.multiple_of(t * n_batch, n_batch), n_batch)
            n_re = a_re * s_re - a_im * s_im + bu_scr[rows, :half]
            n_im = a_re * s_im + a_im * s_re + bu_scr[rows, half:]
            bu_scr[rows, :half] = n_re
            bu_scr[rows, half:] = n_im
            return n_re, n_im

        s_re, s_im = lax.fori_loop(0, chunk, step, (st_re[j], st_im[j]), unroll=8)
        st_re[j] = s_re
        st_im[j] = s_im
        y_scr[:, ch] = jnp.dot(bu_scr[...].astype(BF16), cmat_ref[j], preferred_element_type=F32)

    y = _gelu_tanh(y_scr[...] + dskip_ref[...] * u_scr[...].astype(F32))
    gate = jnp.dot(y.astype(BF16), wglu_ref[...], preferred_element_type=F32) + bglu_ref[...]
    y = (y * jax.nn.sigmoid(gate)).astype(BF16)
    for g in range(n_perm):
        yg = jnp.dot(perm_t_ref[...], y[g * perm_rows:(g + 1) * perm_rows, :], preferred_element_type=F32)
        o_ref[:, g * SSM_PERM_T:(g + 1) * SSM_PERM_T, :] = yg.reshape(
            n_batch, SSM_PERM_T, yg.shape[1]).astype(BF16)


def _ssm_params(lam_re, lam_im, log_dt, b_re, b_im, c_re, c_im):
    groups = lam_re.shape[0]
    per_blk = SSM_CH_BLOCK // SSM_GROUP_CH
    n_blk = groups // per_blk
    dt = jnp.exp(log_dt.astype(F32))[:, None]
    lam_re = lam_re.astype(F32)
    lam_im = lam_im.astype(F32)
    mag = jnp.exp(lam_re * dt)
    bar_re = mag * jnp.cos(lam_im * dt)
    bar_im = mag * jnp.sin(lam_im * dt)
    den = lam_re * lam_re + lam_im * lam_im
    gain_re = ((bar_re - 1.0) * lam_re + bar_im * lam_im) / den
    gain_im = (bar_im * lam_re - (bar_re - 1.0) * lam_im) / den
    bb_re = gain_re[..., None] * b_re - gain_im[..., None] * b_im
    bb_im = gain_re[..., None] * b_im + gain_im[..., None] * b_re
    eye = jnp.eye(per_blk, dtype=F32)

    def in_blocks(m):
        m = m.reshape(n_blk, per_blk, SSM_STATE, SSM_GROUP_CH)
        return jnp.einsum('jgnc,gh->jgchn', m, eye).reshape(n_blk, SSM_CH_BLOCK, per_blk * SSM_STATE)

    def out_blocks(m):
        m = m.reshape(n_blk, per_blk, SSM_GROUP_CH, SSM_STATE)
        return jnp.einsum('jgcn,gh->jhngc', m, eye).reshape(n_blk, per_blk * SSM_STATE, SSM_CH_BLOCK)

    bmat = jnp.concatenate([in_blocks(bb_re), in_blocks(bb_im)], axis=2).astype(BF16)
    cmat = jnp.concatenate([out_blocks(c_re.astype(F32)), out_blocks(-c_im.astype(F32))], axis=1).astype(BF16)
    a_re = bar_re.reshape(n_blk, 1, per_blk * SSM_STATE)
    a_im = bar_im.reshape(n_blk, 1, per_blk * SSM_STATE)
    return bmat, cmat, a_re, a_im


def _time_major_perm(n_batch):
    rows = n_batch * SSM_PERM_T
    perm = np.zeros((rows, rows), np.float32)
    for bi in range(n_batch):
        for tl in range(SSM_PERM_T):
            perm[tl * n_batch + bi, bi * SSM_PERM_T + tl] = 1.0
    return perm


def _ssm_call(x, w_u, bmat, cmat, a_re, a_im, d_skip, w_glu, b_glu):
    b, s, d = x.shape
    width = w_u.shape[1]
    chunk = min(SSM_CHUNK, s)
    rows = b * chunk
    n_blk, _, n_state2 = bmat.shape
    perm = _time_major_perm(b)
    const = lambda *shape: pl.BlockSpec(shape, lambda i: (0,) * len(shape))
    return pl.pallas_call(
        _ssm_kernel,
        grid=(s // chunk,),
        in_specs=[pl.BlockSpec((b, chunk, d), lambda i: (0, i, 0)),
                  const(d, width), const(*perm.shape), const(*perm.shape),
                  const(*bmat.shape), const(*cmat.shape), const(*a_re.shape), const(*a_im.shape),
                  const(1, width), const(width, width), const(1, width)],
        out_specs=pl.BlockSpec((b, chunk, width), lambda i: (0, i, 0)),
        out_shape=jax.ShapeDtypeStruct((b, s, width), BF16),
        scratch_shapes=[pltpu.VMEM((n_blk, b, n_state2 // 2), F32),
                        pltpu.VMEM((n_blk, b, n_state2 // 2), F32),
                        pltpu.VMEM((rows, width), BF16),
                        pltpu.VMEM((rows, n_state2), F32),
                        pltpu.VMEM((rows, width), F32)],
        compiler_params=_params(("arbitrary",)),
        name="s5_mixer",
    )(x, w_u, jnp.asarray(perm, BF16), jnp.asarray(perm.T, BF16), bmat, cmat, a_re, a_im,
      d_skip, w_glu, b_glu)


def _layer_norm(v, g, b):
    mu = jnp.mean(v, axis=-1, keepdims=True)
    c = v - mu
    var = jnp.mean(c * c, axis=-1, keepdims=True)
    return c * lax.rsqrt(var + LN_EPS) * g + b


def _dot_nt(a, b):
    return lax.dot_general(a, b, (((1,), (1,)), ((), ())), preferred_element_type=F32)


def _router_gates(scores, sel):
    rows = scores.shape[1]
    per_group = N_EXPERTS // N_EXPERT_GROUPS
    neg_inf = -jnp.inf
    sel3 = sel.reshape(N_EXPERT_GROUPS, per_group, rows)
    sub = lax.broadcasted_iota(jnp.int32, sel3.shape, 1)
    m1 = jnp.max(sel3, axis=1, keepdims=True)
    first = jnp.min(jnp.where(sel3 == m1, sub, per_group), axis=1, keepdims=True)
    m2 = jnp.max(jnp.where(sub == first, neg_inf, sel3), axis=1, keepdims=True)
    grp = jnp.broadcast_to(m1 + m2, sel3.shape)
    kept = []
    for g in range(N_EXPERT_GROUPS):
        beaten = jnp.zeros(grp.shape[1:], F32)
        for o in range(N_EXPERT_GROUPS):
            if o == g:
                continue
            wins = (grp[o] >= grp[g]) if o < g else (grp[o] > grp[g])
            beaten = beaten + jnp.where(wins, 1.0, 0.0)
        kept.append(jnp.where(beaten < TOPK_GROUPS, sel3[g], neg_inf))
    work = jnp.stack(kept, axis=0).reshape(N_EXPERTS, rows)
    eidx = lax.broadcasted_iota(jnp.int32, work.shape, 0)
    w = jnp.zeros(work.shape, F32)
    for _ in range(TOP_K):
        m = jnp.max(work, axis=0, keepdims=True)
        pick = jnp.min(jnp.where(work == m, eidx, N_EXPERTS), axis=0, keepdims=True)
        hit = eidx == pick
        w = jnp.where(hit, scores, w)
        work = jnp.where(hit, neg_inf, work)
    return w / jnp.sum(w, axis=0, keepdims=True) * ROUTED_SCALE


def _post_kernel(alpha, x_ref, attn_ref, ssm_ref, p_ref, wout_ref, g1_ref, b1_ref, wr_hi_ref, wr_lo_ref,
                 rbias_ref, wsg_ref, wsu_ref, wsd_ref, wple_ref, wpg_ref, r_ref, h16_ref, gates_ref):
    half = attn_ref.shape[1]
    mix = (jnp.dot(attn_ref[...], wout_ref[:half, :], preferred_element_type=F32)
           + jnp.dot(ssm_ref[...], wout_ref[half:, :], preferred_element_type=F32))
    h = _layer_norm(alpha * x_ref[...] + mix, g1_ref[...], b1_ref[...])
    h16 = h.astype(BF16)
    h16_ref[...] = h16
    h_lo = (h - h16.astype(F32)).astype(BF16)
    logits = _dot_nt(wr_hi_ref[...], h16) + _dot_nt(wr_hi_ref[...], h_lo) + _dot_nt(wr_lo_ref[...], h16)
    scores = jax.nn.sigmoid(logits)
    gates = _router_gates(scores, scores + rbias_ref[...])
    gates = jnp.concatenate([gates, jnp.zeros_like(gates)], axis=0)
    gates_ref[...] = gates.T
    sg = jnp.dot(h16, wsg_ref[...], preferred_element_type=F32)
    su = jnp.dot(h16, wsu_ref[...], preferred_element_type=F32)
    shared = jnp.dot((sg * jax.nn.sigmoid(sg) * su).astype(BF16), wsd_ref[...], preferred_element_type=F32)
    ple = (jnp.dot(p_ref[...].astype(BF16), wple_ref[...], preferred_element_type=F32)
           * jax.nn.sigmoid(jnp.dot(h16, wpg_ref[...], preferred_element_type=F32)))
    r_ref[...] = alpha * h + shared + ple


def _post_call(alpha, x2, attn2, ssm2, p2, w_out, g1, b1, wr_hi, wr_lo, rbias, wsg, wsu, wsd, wple, wpg):
    t, d = x2.shape
    rows = min(POST_ROWS, t)
    row_blk = lambda w: pl.BlockSpec((rows, w), lambda i: (i, 0))
    const = lambda a: pl.BlockSpec(a.shape, lambda i: (0,) * a.ndim)
    weights = (w_out, g1, b1, wr_hi, wr_lo, rbias, wsg, wsu, wsd, wple, wpg)
    return pl.pallas_call(
        functools.partial(_post_kernel, alpha),
        grid=(t // rows,),
        in_specs=[row_blk(d), row_blk(attn2.shape[1]), row_blk(ssm2.shape[1]), row_blk(p2.shape[1])]
                 + [const(w) for w in weights],
        out_specs=[row_blk(d), row_blk(d), row_blk(2 * N_EXPERTS)],
        out_shape=[jax.ShapeDtypeStruct((t, d), F32), jax.ShapeDtypeStruct((t, d), BF16),
                   jax.ShapeDtypeStruct((t, 2 * N_EXPERTS), F32)],
        compiler_params=_params(("arbitrary",)),
        name="post_mix_router",
    )(x2, attn2, ssm2, p2, *weights)


def _moe_kernel(h_ref, gates_ref, r_ref, wg_ref, wu_ref, wd_ref, g2_ref, b2_ref, o_ref, acc_ref, hid_ref):
    step = pl.program_id(1)
    n_exp, _, hidden = wg_ref.shape

    @pl.when(step == 0)
    def _():
        acc_ref[...] = jnp.zeros_like(acc_ref)

    lanes = gates_ref.shape[1]
    g = pltpu.roll(gates_ref[...], (lanes - n_exp * step) % lanes, axis=1)
    h = h_ref[...]
    for j in range(n_exp):
        zg = jnp.dot(h, wg_ref[j], preferred_element_type=F32)
        zu = jnp.dot(h, wu_ref[j], preferred_element_type=F32)
        hid = zg * jax.nn.sigmoid(zg) * zu * g[:, j:j + 1]
        hid_ref[:, j * hidden:(j + 1) * hidden] = hid.astype(BF16)
    wd = wd_ref[...].reshape(n_exp * hidden, wd_ref.shape[2])
    acc_ref[...] += jnp.dot(hid_ref[...], wd, preferred_element_type=F32)

    @pl.when(step == pl.num_programs(1) - 1)
    def _():
        o_ref[...] = _layer_norm(r_ref[...] + acc_ref[...], g2_ref[...], b2_ref[...])


def _moe_call(h16, gates, r, wg, wu, wd, g2, b2):
    t, d = h16.shape
    rows = min(MOE_ROWS, t)
    n_exp = MOE_EXPERTS_PER_STEP
    hidden = wg.shape[2]
    row_blk = lambda w: pl.BlockSpec((rows, w), lambda i, e: (i, 0))
    return pl.pallas_call(
        _moe_kernel,
        grid=(t // rows, wg.shape[0] // n_exp),
        in_specs=[row_blk(d), row_blk(gates.shape[1]), row_blk(d),
                  pl.BlockSpec((n_exp, d, hidden), lambda i, e: (e, 0, 0)),
                  pl.BlockSpec((n_exp, d, hidden), lambda i, e: (e, 0, 0)),
                  pl.BlockSpec((n_exp, hidden, d), lambda i, e: (e, 0, 0)),
                  pl.BlockSpec((1, d), lambda i, e: (0, 0)),
                  pl.BlockSpec((1, d), lambda i, e: (0, 0))],
        out_specs=row_blk(d),
        out_shape=jax.ShapeDtypeStruct((t, d), F32),
        scratch_shapes=[pltpu.VMEM((rows, d), F32), pltpu.VMEM((rows, n_exp * hidden), BF16)],
        compiler_params=_params(("arbitrary", "arbitrary")),
        name="routed_experts",
    )(h16, gates, r, wg, wu, wd, g2, b2)


def _layer(h, p_i, w_in, lam_re, lam_im, log_dt, b_re, b_im, c_re, c_im, d_skip, w_glu, b_glu, w_out,
           ln1_g, ln1_b, w_router, router_bias, w_gate, w_up, w_down, ws_gate, ws_up, ws_down, w_ple,
           w_ple_gate, ln2_g, ln2_b, alpha):
    b, s, d = h.shape
    t = b * s
    row = lambda v: v.reshape(1, -1).astype(F32)

    q, k, v = _qkv_call(h, _qkv_weights(w_in), jnp.asarray(_qkv_tables(s), BF16))
    attn = _attn_call(q, k, v, jnp.asarray(_log_multiplicity_table()))

    bmat, cmat, a_re, a_im = _ssm_params(lam_re, lam_im, log_dt, b_re, b_im, c_re, c_im)
    ssm = _ssm_call(h, w_in[:, 3 * ATTN_WIDTH:].astype(BF16), bmat, cmat, a_re, a_im,
                    row(d_skip), w_glu.astype(BF16), row(b_glu))

    wr_t = w_router.astype(F32).T
    wr_hi = wr_t.astype(BF16)
    wr_lo = (wr_t - wr_hi.astype(F32)).astype(BF16)
    r, h16, gates = _post_call(
        alpha, h.reshape(t, d), attn.reshape(t, -1), ssm.reshape(t, -1), p_i.reshape(t, -1),
        w_out.astype(BF16), row(ln1_g), row(ln1_b), wr_hi, wr_lo, router_bias.reshape(-1, 1).astype(F32),
        ws_gate.astype(BF16), ws_up.astype(BF16), ws_down.astype(BF16), w_ple.astype(BF16),
        w_ple_gate.astype(BF16))
    out = _moe_call(h16, gates, r, w_gate.astype(BF16), w_up.astype(BF16), w_down.astype(BF16),
                    row(ln2_g), row(ln2_b))
    return out.reshape(b, s, d)


def kernel(x, p, w_in, lam_re, lam_im, log_dt, b_re, b_im, c_re, c_im, d_skip, w_glu, b_glu, w_out, ln1_g, ln1_b, w_router, router_bias, w_gate, w_up, w_down, ws_gate, ws_up, ws_down, w_ple, w_ple_gate, ln2_g, ln2_b):
    depth = w_in.shape[0]
    alpha = (2.0 * depth) ** 0.25
    h = x
    for i in range(depth):
        h = _layer(h, p[i], w_in[i], lam_re[i], lam_im[i], log_dt[i], b_re[i], b_im[i], c_re[i], c_im[i],
                   d_skip[i], w_glu[i], b_glu[i], w_out[i], ln1_g[i], ln1_b[i], w_router[i], router_bias[i],
                   w_gate[i], w_up[i], w_down[i], ws_gate[i], ws_up[i], ws_down[i], w_ple[i], w_ple_gate[i],
                   ln2_g[i], ln2_b[i], alpha)
    return h
```

```python
import functools
import math

import numpy as np
import jax
import jax.numpy as jnp
from jax import lax
from jax.experimental import pallas as pl
from jax.experimental.pallas import tpu as pltpu

F32 = jnp.float32
BF16 = jnp.bfloat16

HEAD_DIM = 64
N_HEADS = 8
ATTN_WIDTH = N_HEADS * HEAD_DIM
HEAD_LANES = 128
SSM_GROUP_CH = 16
SSM_STATE = 64
DILATED_BRANCHES = ((128, 1), (512, 4), (2048, 16))
N_EXPERTS = 64
TOP_K = 8
N_EXPERT_GROUPS = 8
TOPK_GROUPS = 4
ROUTED_SCALE = 2.5
LN_EPS = 1e-5
MASK_VALUE = -1e30

ATT_BLOCK = 256
ATT_LOOKBACK = max(w for w, _ in DILATED_BRANCHES) // ATT_BLOCK
QKV_ROWS = 512
SSM_CHUNK = 128
SSM_PERM_T = 32
SSM_CH_BLOCK = 128
POST_ROWS = 512
MOE_ROWS = 1024
MOE_EXPERTS_PER_STEP = 4
VMEM_LIMIT = 56 * 1024 * 1024


def _params(sem, vmem=VMEM_LIMIT):
    return pltpu.CompilerParams(dimension_semantics=sem, vmem_limit_bytes=vmem)


def _qkv_kernel(x_ref, w_ref, pos_ref, q_ref, k_ref, v_ref):
    xb = x_ref[0].astype(BF16)
    width = N_HEADS * HEAD_LANES
    for c, o_ref in enumerate((q_ref, k_ref, v_ref)):
        cols = slice(c * width, (c + 1) * width)
        z = jnp.dot(xb, w_ref[:, cols], preferred_element_type=F32)
        z = (z + pos_ref[:, cols].astype(F32)).astype(BF16)
        for h in range(N_HEADS):
            o_ref[0, h] = z[:, h * HEAD_LANES:(h + 1) * HEAD_LANES]


def _qkv_tables(seq):
    t = np.arange(seq)
    width = N_HEADS * HEAD_LANES
    qpos = np.zeros((seq, width), np.float32)
    kpos = np.zeros((seq, width), np.float32)
    vpos = np.zeros((seq, width), np.float32)
    for h in range(N_HEADS):
        slope = 2.0 ** (-8.0 * (h + 1) / N_HEADS)
        base = h * HEAD_LANES + HEAD_DIM
        qpos[:, base + 0] = -slope * (t % ATT_BLOCK)
        qpos[:, base + 1] = -slope * ATT_BLOCK * (t // ATT_BLOCK)
        qpos[:, base + 2] = slope
        qpos[:, base + 3] = slope * ATT_BLOCK
        kpos[:, base + 0] = 1.0
        kpos[:, base + 1] = 1.0
        kpos[:, base + 2] = t % ATT_BLOCK
        kpos[:, base + 3] = t // ATT_BLOCK
        ones_lane = h * HEAD_LANES + (HEAD_DIM if h % 2 == 0 else 0)
        vpos[:, ones_lane] = 1.0
    return np.concatenate([qpos, kpos, vpos], axis=1)


def _qkv_weights(w_in):
    d = w_in.shape[0]
    pad_hi = ((0, 0), (0, 0), (0, HEAD_LANES - HEAD_DIM))
    pad_lo = ((0, 0), (0, 0), (HEAD_LANES - HEAD_DIM, 0))
    wq = w_in[:, :ATTN_WIDTH].reshape(d, N_HEADS, HEAD_DIM) * (HEAD_DIM ** -0.5)
    wk = w_in[:, ATTN_WIDTH:2 * ATTN_WIDTH].reshape(d, N_HEADS, HEAD_DIM)
    wv = w_in[:, 2 * ATTN_WIDTH:3 * ATTN_WIDTH].reshape(d, N_HEADS // 2, 2, HEAD_DIM)
    wq = jnp.pad(wq, pad_hi).reshape(d, -1)
    wk = jnp.pad(wk, pad_hi).reshape(d, -1)
    wv = jnp.stack([jnp.pad(wv[:, :, 0], pad_hi), jnp.pad(wv[:, :, 1], pad_lo)], axis=2).reshape(d, -1)
    return jnp.concatenate([wq, wk, wv], axis=1).astype(BF16)


def _qkv_call(x, w_qkv, pos):
    b, s, d = x.shape
    width = N_HEADS * HEAD_LANES
    rows = min(QKV_ROWS, s)
    out = jax.ShapeDtypeStruct((b, N_HEADS, s, HEAD_LANES), BF16)
    blk = pl.BlockSpec((1, N_HEADS, rows, HEAD_LANES), lambda si, bi: (bi, 0, si, 0))
    return pl.pallas_call(
        _qkv_kernel,
        grid=(s // rows, b),
        in_specs=[pl.BlockSpec((1, rows, d), lambda si, bi: (bi, si, 0)),
                  pl.BlockSpec((d, 3 * width), lambda si, bi: (0, 0)),
                  pl.BlockSpec((rows, 3 * width), lambda si, bi: (si, 0))],
        out_specs=[blk, blk, blk],
        out_shape=[out, out, out],
        compiler_params=_params(("arbitrary", "arbitrary")),
        name="qkv_proj",
    )(x, w_qkv, pos)


def _log_multiplicity_table():
    a = (ATT_LOOKBACK - np.arange(2 * ATT_LOOKBACK + 1))[:, None, None]
    i = np.arange(ATT_BLOCK)[None, :, None]
    j = np.arange(ATT_BLOCK)[None, None, :]
    dist = ATT_BLOCK * a + i - j
    mult = np.zeros(dist.shape, np.float64)
    for window, dil in DILATED_BRANCHES:
        mult += (dist >= 0) & (dist <= window) & (dist % dil == 0)
    return np.where(mult > 0, np.log(np.maximum(mult, 1.0)), MASK_VALUE).astype(np.float32)


def _dot_nt(a, b):
    return lax.dot_general(a, b, (((1,), (1,)), ((), ())), preferred_element_type=F32)


def _attn_kernel(n_win, q_ref, k_ref, v_ref, bias_ref, o_ref):
    qi = pl.program_id(1)
    first = jnp.maximum(qi - (n_win - 1), 0)
    start = pl.multiple_of(first * ATT_BLOCK, ATT_BLOCK)
    u0 = ATT_LOOKBACK - (qi - first)
    lane = lax.broadcasted_iota(jnp.int32, (ATT_BLOCK, HEAD_LANES), 1)

    def pair_body(pair, carry):
        outs = []
        for e in range(2):
            h = 2 * pair + e
            kw = k_ref[0, h, pl.ds(start, n_win * ATT_BLOCK), :]
            vw = v_ref[0, h, pl.ds(start, n_win * ATT_BLOCK), :]
            s = _dot_nt(q_ref[0, h], kw)
            s = s + jnp.concatenate([bias_ref[u0 + c] for c in range(n_win)], axis=1)
            p = jnp.exp(s - jnp.max(s, axis=1, keepdims=True))
            acc = jnp.dot(p.astype(BF16), vw, preferred_element_type=F32)
            ones_lane = HEAD_DIM if e == 0 else 0
            outs.append(acc * (1.0 / acc[:, ones_lane:ones_lane + 1]))
        o_ref[0, pair] = jnp.where(lane < HEAD_DIM, outs[0], outs[1]).astype(BF16)
        return carry

    lax.fori_loop(0, N_HEADS // 2, pair_body, 0)


def _attn_call(q, k, v, bias):
    b, n_heads, s, lanes = q.shape
    n_win = min(ATT_LOOKBACK + 1, s // ATT_BLOCK)
    once = pl.Buffered(1)
    return pl.pallas_call(
        functools.partial(_attn_kernel, n_win),
        grid=(b, s // ATT_BLOCK),
        in_specs=[pl.BlockSpec((1, n_heads, ATT_BLOCK, lanes), lambda bi, qi: (bi, 0, qi, 0)),
                  pl.BlockSpec((1, n_heads, s, lanes), lambda bi, qi: (bi, 0, 0, 0), pipeline_mode=once),
                  pl.BlockSpec((1, n_heads, s, lanes), lambda bi, qi: (bi, 0, 0, 0), pipeline_mode=once),
                  pl.BlockSpec(bias.shape, lambda bi, qi: (0, 0, 0), pipeline_mode=once)],
        out_specs=pl.BlockSpec((1, n_heads // 2, ATT_BLOCK, lanes), lambda bi, qi: (bi, 0, qi, 0)),
        out_shape=jax.ShapeDtypeStruct((b, n_heads // 2, s, lanes), BF16),
        compiler_params=_params(("arbitrary", "arbitrary")),
        name="banded_attention",
    )(q, k, v, bias)


def _gelu_tanh(y):
    return 0.5 * y * (1.0 + jnp.tanh(math.sqrt(2.0 / math.pi) * (y + 0.044715 * (y * y * y))))


def _ssm_kernel(x_ref, wu_ref, perm_ref, perm_t_ref, bmat_ref, cmat_ref, a_re_ref, a_im_ref, dskip_ref,
                wglu_ref, bglu_ref, o_ref, st_re, st_im, u_scr, bu_scr, y_scr):
    n_batch, chunk, _ = x_ref.shape
    n_perm = chunk // SSM_PERM_T
    perm_rows = n_batch * SSM_PERM_T
    n_blocks = bmat_ref.shape[0]
    half = bmat_ref.shape[2] // 2

    @pl.when(pl.program_id(0) == 0)
    def _():
        st_re[...] = jnp.zeros_like(st_re)
        st_im[...] = jnp.zeros_like(st_im)

    for g in range(n_perm):
        xg = x_ref[:, g * SSM_PERM_T:(g + 1) * SSM_PERM_T, :].reshape(perm_rows, x_ref.shape[2])
        ug = jnp.dot(xg.astype(BF16), wu_ref[...], preferred_element_type=F32).astype(BF16)
        u_scr[g * perm_rows:(g + 1) * perm_rows, :] = jnp.dot(
            perm_ref[...], ug, preferred_element_type=F32).astype(BF16)

    for j in range(n_blocks):
        ch = slice(j * SSM_CH_BLOCK, (j + 1) * SSM_CH_BLOCK)
        bu_scr[...] = jnp.dot(u_scr[:, ch], bmat_ref[j], preferred_element_type=F32)
        a_re = jnp.broadcast_to(a_re_ref[j], (n_batch, half))
        a_im = jnp.broadcast_to(a_im_ref[j], (n_batch, half))

        def step(t, carry, a_re=a_re, a_im=a_im):
            s_re, s_im = carry
            rows = pl.ds(pl.multiple_of(t * n_batch, n_batch), n_batch)
            n_re = a_re * s_re - a_im * s_im + bu_scr[rows, :half]
            n_im = a_re * s_im + a_im * s_re + bu_scr[rows, half:]
            bu_scr[rows, :half] = n_re
            bu_scr[rows, half:] = n_im
            return n_re, n_im

        s_re, s_im = lax.fori_loop(0, chunk, step, (st_re[j], st_im[j]), unroll=8)
        st_re[j] = s_re
        st_im[j] = s_im
        y_scr[:, ch] = jnp.dot(bu_scr[...].astype(BF16), cmat_ref[j], preferred_element_type=F32)

    y = _gelu_tanh(y_scr[...] + dskip_ref[...] * u_scr[...].astype(F32))
    gate = jnp.dot(y.astype(BF16), wglu_ref[...], preferred_element_type=F32) + bglu_ref[...]
    y = (y * jax.nn.sigmoid(gate)).astype(BF16)
    for g in range(n_perm):
        yg = jnp.dot(perm_t_ref[...], y[g * perm_rows:(g + 1) * perm_rows, :], preferred_element_type=F32)
        o_ref[:, g * SSM_PERM_T:(g + 1) * SSM_PERM_T, :] = yg.reshape(
            n_batch, SSM_PERM_T, yg.shape[1]).astype(BF16)


def _ssm_params(lam_re, lam_im, log_dt, b_re, b_im, c_re, c_im):
    groups = lam_re.shape[0]
    per_blk = SSM_CH_BLOCK // SSM_GROUP_CH
    n_blk = groups // per_blk
    dt = jnp.exp(log_dt.astype(F32))[:, None]
    lam_re = lam_re.astype(F32)
    lam_im = lam_im.astype(F32)
    mag = jnp.exp(lam_re * dt)
    bar_re = mag * jnp.cos(lam_im * dt)
    bar_im = mag * jnp.sin(lam_im * dt)
    den = lam_re * lam_re + lam_im * lam_im
    gain_re = ((bar_re - 1.0) * lam_re + bar_im * lam_im) / den
    gain_im = (bar_im * lam_re - (bar_re - 1.0) * lam_im) / den
    bb_re = gain_re[..., None] * b_re - gain_im[..., None] * b_im
    bb_im = gain_re[..., None] * b_im + gain_im[..., None] * b_re
    eye = jnp.eye(per_blk, dtype=F32)

    def in_blocks(m):
        m = m.reshape(n_blk, per_blk, SSM_STATE, SSM_GROUP_CH)
        return jnp.einsum('jgnc,gh->jgchn', m, eye).reshape(n_blk, SSM_CH_BLOCK, per_blk * SSM_STATE)

    def out_blocks(m):
        m = m.reshape(n_blk, per_blk, SSM_GROUP_CH, SSM_STATE)
        return jnp.einsum('jgcn,gh->jhngc', m, eye).reshape(n_blk, per_blk * SSM_STATE, SSM_CH_BLOCK)

    bmat = jnp.concatenate([in_blocks(bb_re), in_blocks(bb_im)], axis=2).astype(BF16)
    cmat = jnp.concatenate([out_blocks(c_re.astype(F32)), out_blocks(-c_im.astype(F32))], axis=1).astype(BF16)
    a_re = bar_re.reshape(n_blk, 1, per_blk * SSM_STATE)
    a_im = bar_im.reshape(n_blk, 1, per_blk * SSM_STATE)
    return bmat, cmat, a_re, a_im


def _time_major_perm(n_batch):
    rows = n_batch * SSM_PERM_T
    perm = np.zeros((rows, rows), np.float32)
    for bi in range(n_batch):
        for tl in range(SSM_PERM_T):
            perm[tl * n_batch + bi, bi * SSM_PERM_T + tl] = 1.0
    return perm


def _ssm_call(x, w_u, bmat, cmat, a_re, a_im, d_skip, w_glu, b_glu):
    b, s, d = x.shape
    width = w_u.shape[1]
    chunk = min(SSM_CHUNK, s)
    rows = b * chunk
    n_blk, _, n_state2 = bmat.shape
    perm = _time_major_perm(b)
    const = lambda *shape: pl.BlockSpec(shape, lambda i: (0,) * len(shape))
    return pl.pallas_call(
        _ssm_kernel,
        grid=(s // chunk,),
        in_specs=[pl.BlockSpec((b, chunk, d), lambda i: (0, i, 0)),
                  const(d, width), const(*perm.shape), const(*perm.shape),
                  const(*bmat.shape), const(*cmat.shape), const(*a_re.shape), const(*a_im.shape),
                  const(1, width), const(width, width), const(1, width)],
        out_specs=pl.BlockSpec((b, chunk, width), lambda i: (0, i, 0)),
        out_shape=jax.ShapeDtypeStruct((b, s, width), BF16),
        scratch_shapes=[pltpu.VMEM((n_blk, b, n_state2 // 2), F32),
                        pltpu.VMEM((n_blk, b, n_state2 // 2), F32),
                        pltpu.VMEM((rows, width), BF16),
                        pltpu.VMEM((rows, n_state2), F32),
                        pltpu.VMEM((rows, width), F32)],
        compiler_params=_params(("arbitrary",)),
        name="s5_mixer",
    )(x, w_u, jnp.asarray(perm, BF16), jnp.asarray(perm.T, BF16), bmat, cmat, a_re, a_im,
      d_skip, w_glu, b_glu)


def _layer_norm(v, g, b):
    mu = jnp.mean(v, axis=-1, keepdims=True)
    c = v - mu
    var = jnp.mean(c * c, axis=-1, keepdims=True)
    return c * lax.rsqrt(var + LN_EPS) * g + b


def _router_gates(scores, sel):
    rows = scores.shape[1]
    per_group = N_EXPERTS // N_EXPERT_GROUPS
    neg_inf = -jnp.inf
    sel3 = sel.reshape(N_EXPERT_GROUPS, per_group, rows)
    sub = lax.broadcasted_iota(jnp.int32, sel3.shape, 1)
    m1 = jnp.max(sel3, axis=1, keepdims=True)
    first = jnp.min(jnp.where(sel3 == m1, sub, per_group), axis=1, keepdims=True)
    m2 = jnp.max(jnp.where(sub == first, neg_inf, sel3), axis=1, keepdims=True)
    grp = jnp.broadcast_to(m1 + m2, sel3.shape)
    kept = []
    for g in range(N_EXPERT_GROUPS):
        beaten = jnp.zeros(grp.shape[1:], F32)
        for o in range(N_EXPERT_GROUPS):
            if o == g:
                continue
            wins = (grp[o] >= grp[g]) if o < g else (grp[o] > grp[g])
            beaten = beaten + jnp.where(wins, 1.0, 0.0)
        kept.append(jnp.where(beaten < TOPK_GROUPS, sel3[g], neg_inf))
    work = jnp.stack(kept, axis=0).reshape(N_EXPERTS, rows)
    eidx = lax.broadcasted_iota(jnp.int32, work.shape, 0)
    w = jnp.zeros(work.shape, F32)
    for _ in range(TOP_K):
        m = jnp.max(work, axis=0, keepdims=True)
        pick = jnp.min(jnp.where(work == m, eidx, N_EXPERTS), axis=0, keepdims=True)
        hit = eidx == pick
        w = jnp.where(hit, scores, w)
        work = jnp.where(hit, neg_inf, work)
    return w / jnp.sum(w, axis=0, keepdims=True) * ROUTED_SCALE


def _post_kernel(alpha, x_ref, attn_ref, ssm_ref, p_ref, wout_ref, g1_ref, b1_ref, wr_hi_ref, wr_lo_ref,
                 rbias_ref, wsg_ref, wsu_ref, wsd_ref, wple_ref, wpg_ref, r_ref, h16_ref, gates_ref):
    n_pairs, _, lanes = attn_ref.shape[1:]
    mix = jnp.dot(ssm_ref[...], wout_ref[n_pairs * lanes:, :], preferred_element_type=F32)
    for pair in range(n_pairs):
        mix = mix + jnp.dot(attn_ref[0, pair], wout_ref[pair * lanes:(pair + 1) * lanes, :],
                            preferred_element_type=F32)
    h = _layer_norm(alpha * x_ref[...] + mix, g1_ref[...], b1_ref[...])
    h16 = h.astype(BF16)
    h16_ref[...] = h16
    h_lo = (h - h16.astype(F32)).astype(BF16)
    logits = _dot_nt(wr_hi_ref[...], h16) + _dot_nt(wr_hi_ref[...], h_lo) + _dot_nt(wr_lo_ref[...], h16)
    scores = jax.nn.sigmoid(logits)
    gates = _router_gates(scores, scores + rbias_ref[...])
    gates = jnp.concatenate([gates, jnp.zeros_like(gates)], axis=0)
    gates_ref[...] = gates.T
    sg = jnp.dot(h16, wsg_ref[...], preferred_element_type=F32)
    su = jnp.dot(h16, wsu_ref[...], preferred_element_type=F32)
    shared = jnp.dot((sg * jax.nn.sigmoid(sg) * su).astype(BF16), wsd_ref[...], preferred_element_type=F32)
    ple = (jnp.dot(p_ref[...].astype(BF16), wple_ref[...], preferred_element_type=F32)
           * jax.nn.sigmoid(jnp.dot(h16, wpg_ref[...], preferred_element_type=F32)))
    r_ref[...] = alpha * h + shared + ple


def _post_call(alpha, x2, attn, ssm2, p2, w_out, g1, b1, wr_hi, wr_lo, rbias, wsg, wsu, wsd, wple, wpg):
    t, d = x2.shape
    _, n_pairs, s, lanes = attn.shape
    rows = min(POST_ROWS, s)
    per_seq = s // rows
    row_blk = lambda w: pl.BlockSpec((rows, w), lambda i: (i, 0))
    const = lambda a: pl.BlockSpec(a.shape, lambda i: (0,) * a.ndim)
    attn_blk = pl.BlockSpec((1, n_pairs, rows, lanes), lambda i: (i // per_seq, 0, i % per_seq, 0))
    weights = (w_out, g1, b1, wr_hi, wr_lo, rbias, wsg, wsu, wsd, wple, wpg)
    return pl.pallas_call(
        functools.partial(_post_kernel, alpha),
        grid=(t // rows,),
        in_specs=[row_blk(d), attn_blk, row_blk(ssm2.shape[1]), row_blk(p2.shape[1])]
                 + [const(w) for w in weights],
        out_specs=[row_blk(d), row_blk(d), row_blk(2 * N_EXPERTS)],
        out_shape=[jax.ShapeDtypeStruct((t, d), F32), jax.ShapeDtypeStruct((t, d), BF16),
                   jax.ShapeDtypeStruct((t, 2 * N_EXPERTS), F32)],
        compiler_params=_params(("arbitrary",)),
        name="post_mix_router",
    )(x2, attn, ssm2, p2, *weights)


def _moe_kernel(h_ref, gates_ref, r_ref, wg_ref, wu_ref, wd_ref, g2_ref, b2_ref, o_ref, acc_ref, hid_ref):
    step = pl.program_id(1)
    n_exp, _, hidden = wg_ref.shape

    @pl.when(step == 0)
    def _():
        acc_ref[...] = jnp.zeros_like(acc_ref)

    lanes = gates_ref.shape[1]
    g = pltpu.roll(gates_ref[...], (lanes - n_exp * step) % lanes, axis=1)
    h = h_ref[...]
    for j in range(n_exp):
        zg = jnp.dot(h, wg_ref[j], preferred_element_type=F32)
        zu = jnp.dot(h, wu_ref[j], preferred_element_type=F32)
        hid = zg * jax.nn.sigmoid(zg) * zu * g[:, j:j + 1]
        hid_ref[:, j * hidden:(j + 1) * hidden] = hid.astype(BF16)
    wd = wd_ref[...].reshape(n_exp * hidden, wd_ref.shape[2])
    acc_ref[...] += jnp.dot(hid_ref[...], wd, preferred_element_type=F32)

    @pl.when(step == pl.num_programs(1) - 1)
    def _():
        o_ref[...] = _layer_norm(r_ref[...] + acc_ref[...], g2_ref[...], b2_ref[...])


def _moe_call(h16, gates, r, wg, wu, wd, g2, b2):
    t, d = h16.shape
    rows = min(MOE_ROWS, t)
    n_exp = MOE_EXPERTS_PER_STEP
    hidden = wg.shape[2]
    row_blk = lambda w: pl.BlockSpec((rows, w), lambda i, e: (i, 0))
    return pl.pallas_call(
        _moe_kernel,
        grid=(t // rows, wg.shape[0] // n_exp),
        in_specs=[row_blk(d), row_blk(gates.shape[1]), row_blk(d),
                  pl.BlockSpec((n_exp, d, hidden), lambda i, e: (e, 0, 0)),
                  pl.BlockSpec((n_exp, d, hidden), lambda i, e: (e, 0, 0)),
                  pl.BlockSpec((n_exp, hidden, d), lambda i, e: (e, 0, 0)),
                  pl.BlockSpec((1, d), lambda i, e: (0, 0)),
                  pl.BlockSpec((1, d), lambda i, e: (0, 0))],
        out_specs=row_blk(d),
        out_shape=jax.ShapeDtypeStruct((t, d), F32),
        scratch_shapes=[pltpu.VMEM((rows, d), F32), pltpu.VMEM((rows, n_exp * hidden), BF16)],
        compiler_params=_params(("arbitrary", "arbitrary")),
        name="routed_experts",
    )(h16, gates, r, wg, wu, wd, g2, b2)


def _layer(h, p_i, w_in, lam_re, lam_im, log_dt, b_re, b_im, c_re, c_im, d_skip, w_glu, b_glu, w_out,
           ln1_g, ln1_b, w_router, router_bias, w_gate, w_up, w_down, ws_gate, ws_up, ws_down, w_ple,
           w_ple_gate, ln2_g, ln2_b, alpha):
    b, s, d = h.shape
    t = b * s
    row = lambda v: v.reshape(1, -1).astype(F32)

    q, k, v = _qkv_call(h, _qkv_weights(w_in), jnp.asarray(_qkv_tables(s), BF16))
    attn = _attn_call(q, k, v, jnp.asarray(_log_multiplicity_table()))

    bmat, cmat, a_re, a_im = _ssm_params(lam_re, lam_im, log_dt, b_re, b_im, c_re, c_im)
    ssm = _ssm_call(h, w_in[:, 3 * ATTN_WIDTH:].astype(BF16), bmat, cmat, a_re, a_im,
                    row(d_skip), w_glu.astype(BF16), row(b_glu))

    wr_t = w_router.astype(F32).T
    wr_hi = wr_t.astype(BF16)
    wr_lo = (wr_t - wr_hi.astype(F32)).astype(BF16)
    r, h16, gates = _post_call(
        alpha, h.reshape(t, d), attn, ssm.reshape(t, -1), p_i.reshape(t, -1),
        w_out.astype(BF16), row(ln1_g), row(ln1_b), wr_hi, wr_lo, router_bias.reshape(-1, 1).astype(F32),
        ws_gate.astype(BF16), ws_up.astype(BF16), ws_down.astype(BF16), w_ple.astype(BF16),
        w_ple_gate.astype(BF16))
    out = _moe_call(h16, gates, r, w_gate.astype(BF16), w_up.astype(BF16), w_down.astype(BF16),
                    row(ln2_g), row(ln2_b))
    return out.reshape(b, s, d)


def kernel(x, p, w_in, lam_re, lam_im, log_dt, b_re, b_im, c_re, c_im, d_skip, w_glu, b_glu, w_out, ln1_g, ln1_b, w_router, router_bias, w_gate, w_up, w_down, ws_gate, ws_up, ws_down, w_ple, w_ple_gate, ln2_g, ln2_b):
    depth = w_in.shape[0]
    alpha = (2.0 * depth) ** 0.25
    h = x
    for i in range(depth):
        h = _layer(h, p[i], w_in[i], lam_re[i], lam_im[i], log_dt[i], b_re[i], b_im[i], c_re[i], c_im[i],
                   d_skip[i], w_glu[i], b_glu[i], w_out[i], ln1_g[i], ln1_b[i], w_router[i], router_bias[i],
                   w_gate[i], w_up[i], w_down[i], ws_gate[i], ws_up[i], ws_down[i], w_ple[i], w_ple_gate[i],
                   ln2_g[i], ln2_b[i], alpha)
    return h
```

```python
import functools
import math

import numpy as np
import jax
import jax.numpy as jnp
from jax import lax
from jax.experimental import pallas as pl
from jax.experimental.pallas import tpu as pltpu

F32 = jnp.float32
BF16 = jnp.bfloat16

HEAD_DIM = 64
N_HEADS = 8
ATTN_WIDTH = N_HEADS * HEAD_DIM
HEAD_LANES = 128
SSM_GROUP_CH = 16
SSM_STATE = 64
DILATED_BRANCHES = ((128, 1), (512, 4), (2048, 16))
N_EXPERTS = 64
TOP_K = 8
N_EXPERT_GROUPS = 8
TOPK_GROUPS = 4
ROUTED_SCALE = 2.5
LN_EPS = 1e-5
MASK_VALUE = -1e30

ATT_BLOCK = 256
ATT_LOOKBACK = max(w for w, _ in DILATED_BRANCHES) // ATT_BLOCK
ATT_WINDOWS = (3, 6, ATT_LOOKBACK + 1)
QKV_ROWS = 512
SSM_CHUNK = 128
SSM_PERM_T = 32
SSM_CH_BLOCK = 128
POST_ROWS = 512
MOE_ROWS = 1024
MOE_EXPERTS_PER_STEP = 4
VMEM_LIMIT = 56 * 1024 * 1024


def _params(sem, vmem=VMEM_LIMIT):
    return pltpu.CompilerParams(dimension_semantics=sem, vmem_limit_bytes=vmem)


def _qkv_kernel(x_ref, w_ref, pos_ref, q_ref, k_ref, v_ref):
    z = jnp.dot(x_ref[0].astype(BF16), w_ref[...], preferred_element_type=F32).astype(BF16)
    lane = lax.broadcasted_iota(jnp.int32, (z.shape[0], HEAD_LANES), 1)
    for c, o_ref in enumerate((q_ref, k_ref, v_ref)):
        for h in range(N_HEADS):
            zc = z[:, c * ATTN_WIDTH + (h // 2) * HEAD_LANES:c * ATTN_WIDTH + (h // 2 + 1) * HEAD_LANES]
            pos = pos_ref[:, (c * N_HEADS + h) * HEAD_LANES:(c * N_HEADS + h + 1) * HEAD_LANES]
            data = (lane < HEAD_DIM) if h % 2 == 0 else (lane >= HEAD_DIM)
            o_ref[0, h] = jnp.where(data, zc, pos)


def _qkv_tables(seq):
    t = np.arange(seq)
    width = N_HEADS * HEAD_LANES
    qpos = np.zeros((seq, width), np.float32)
    kpos = np.zeros((seq, width), np.float32)
    vpos = np.zeros((seq, width), np.float32)
    for h in range(N_HEADS):
        slope = 2.0 ** (-8.0 * (h + 1) / N_HEADS)
        base = h * HEAD_LANES + (HEAD_DIM if h % 2 == 0 else 0)
        qpos[:, base + 0] = -slope * (t % ATT_BLOCK)
        qpos[:, base + 1] = -slope * ATT_BLOCK * (t // ATT_BLOCK)
        qpos[:, base + 2] = slope
        qpos[:, base + 3] = slope * ATT_BLOCK
        kpos[:, base + 0] = 1.0
        kpos[:, base + 1] = 1.0
        kpos[:, base + 2] = t % ATT_BLOCK
        kpos[:, base + 3] = t // ATT_BLOCK
        ones_lane = h * HEAD_LANES + (HEAD_DIM if h % 2 == 0 else 0)
        vpos[:, ones_lane] = 1.0
    return np.concatenate([qpos, kpos, vpos], axis=1)


def _qkv_weights(w_in):
    wq = w_in[:, :ATTN_WIDTH] * (HEAD_DIM ** -0.5)
    return jnp.concatenate([wq, w_in[:, ATTN_WIDTH:3 * ATTN_WIDTH]], axis=1).astype(BF16)


def _qkv_call(x, w_qkv, pos):
    b, s, d = x.shape
    width = N_HEADS * HEAD_LANES
    rows = min(QKV_ROWS, s)
    out = jax.ShapeDtypeStruct((b, N_HEADS, s, HEAD_LANES), BF16)
    blk = pl.BlockSpec((1, N_HEADS, rows, HEAD_LANES), lambda si, bi: (bi, 0, si, 0))
    return pl.pallas_call(
        _qkv_kernel,
        grid=(s // rows, b),
        in_specs=[pl.BlockSpec((1, rows, d), lambda si, bi: (bi, si, 0)),
                  pl.BlockSpec(w_qkv.shape, lambda si, bi: (0, 0)),
                  pl.BlockSpec((rows, 3 * width), lambda si, bi: (si, 0))],
        out_specs=[blk, blk, blk],
        out_shape=[out, out, out],
        compiler_params=_params(("arbitrary", "arbitrary")),
        name="qkv_proj",
    )(x, w_qkv, pos)


def _log_multiplicity_table():
    a = (ATT_LOOKBACK - np.arange(2 * ATT_LOOKBACK + 1))[:, None, None]
    i = np.arange(ATT_BLOCK)[None, :, None]
    j = np.arange(ATT_BLOCK)[None, None, :]
    dist = ATT_BLOCK * a + i - j
    mult = np.zeros(dist.shape, np.float64)
    for window, dil in DILATED_BRANCHES:
        mult += (dist >= 0) & (dist <= window) & (dist % dil == 0)
    return np.where(mult > 0, np.log(np.maximum(mult, 1.0)), MASK_VALUE).astype(np.float32)


def _dot_nt(a, b):
    return lax.dot_general(a, b, (((1,), (1,)), ((), ())), preferred_element_type=F32)


def _attn_kernel(windows, q_ref, k_ref, v_ref, bias_ref, o_ref):
    qi = pl.program_id(1)
    lo = 0
    for idx, n_win in enumerate(windows):
        last = idx == len(windows) - 1
        cond = (qi >= lo) if last else jnp.logical_and(qi >= lo, qi < n_win)
        pl.when(cond)(functools.partial(_attn_window, n_win, q_ref, k_ref, v_ref, bias_ref, o_ref))
        lo = n_win


def _attn_window(n_win, q_ref, k_ref, v_ref, bias_ref, o_ref):
    qi = pl.program_id(1)
    first = jnp.maximum(qi - (n_win - 1), 0)
    start = pl.multiple_of(first * ATT_BLOCK, ATT_BLOCK)
    u0 = ATT_LOOKBACK - (qi - first)
    lane = lax.broadcasted_iota(jnp.int32, (ATT_BLOCK, HEAD_LANES), 1)

    def pair_body(pair, carry):
        outs = []
        for e in range(2):
            h = 2 * pair + e
            kw = k_ref[0, h, pl.ds(start, n_win * ATT_BLOCK), :]
            vw = v_ref[0, h, pl.ds(start, n_win * ATT_BLOCK), :]
            s = _dot_nt(q_ref[0, h], kw)
            s = s + jnp.concatenate([bias_ref[u0 + c] for c in range(n_win)], axis=1)
            p = jnp.exp(s - jnp.max(s, axis=1, keepdims=True))
            acc = jnp.dot(p.astype(BF16), vw, preferred_element_type=F32)
            ones_lane = HEAD_DIM if e == 0 else 0
            outs.append(acc * (1.0 / acc[:, ones_lane:ones_lane + 1]))
        o_ref[0, pair] = jnp.where(lane < HEAD_DIM, outs[0], outs[1]).astype(BF16)
        return carry

    lax.fori_loop(0, N_HEADS // 2, pair_body, 0)


def _attn_call(q, k, v, bias):
    b, n_heads, s, lanes = q.shape
    n_win = min(ATT_LOOKBACK + 1, s // ATT_BLOCK)
    windows = tuple(sorted({min(w, n_win) for w in ATT_WINDOWS}))
    once = pl.Buffered(1)
    return pl.pallas_call(
        functools.partial(_attn_kernel, windows),
        grid=(b, s // ATT_BLOCK),
        in_specs=[pl.BlockSpec((1, n_heads, ATT_BLOCK, lanes), lambda bi, qi: (bi, 0, qi, 0)),
                  pl.BlockSpec((1, n_heads, s, lanes), lambda bi, qi: (bi, 0, 0, 0), pipeline_mode=once),
                  pl.BlockSpec((1, n_heads, s, lanes), lambda bi, qi: (bi, 0, 0, 0), pipeline_mode=once),
                  pl.BlockSpec(bias.shape, lambda bi, qi: (0, 0, 0), pipeline_mode=once)],
        out_specs=pl.BlockSpec((1, n_heads // 2, ATT_BLOCK, lanes), lambda bi, qi: (bi, 0, qi, 0)),
        out_shape=jax.ShapeDtypeStruct((b, n_heads // 2, s, lanes), BF16),
        compiler_params=_params(("arbitrary", "arbitrary")),
        name="banded_attention",
    )(q, k, v, bias)


def _gelu_tanh(y):
    return 0.5 * y * (1.0 + jnp.tanh(math.sqrt(2.0 / math.pi) * (y + 0.044715 * (y * y * y))))


def _ssm_kernel(x_ref, wu_ref, perm_ref, perm_t_ref, bmat_ref, cmat_ref, a_re_ref, a_im_ref, dskip_ref,
                wglu_ref, bglu_ref, o_ref, st_re, st_im, u_scr, bu_scr, y_scr):
    n_batch, chunk, _ = x_ref.shape
    n_perm = chunk // SSM_PERM_T
    perm_rows = n_batch * SSM_PERM_T
    n_blocks = bmat_ref.shape[0]
    half = bmat_ref.shape[2] // 2

    @pl.when(pl.program_id(0) == 0)
    def _():
        st_re[...] = jnp.zeros_like(st_re)
        st_im[...] = jnp.zeros_like(st_im)

    for g in range(n_perm):
        xg = x_ref[:, g * SSM_PERM_T:(g + 1) * SSM_PERM_T, :].reshape(perm_rows, x_ref.shape[2])
        ug = jnp.dot(xg.astype(BF16), wu_ref[...], preferred_element_type=F32).astype(BF16)
        u_scr[g * perm_rows:(g + 1) * perm_rows, :] = jnp.dot(
            perm_ref[...], ug, preferred_element_type=F32).astype(BF16)

    for j in range(n_blocks):
        ch = slice(j * SSM_CH_BLOCK, (j + 1) * SSM_CH_BLOCK)
        bu_scr[...] = jnp.dot(u_scr[:, ch], bmat_ref[j], preferred_element_type=F32)
        a_re = jnp.broadcast_to(a_re_ref[j], (n_batch, half))
        a_im = jnp.broadcast_to(a_im_ref[j], (n_batch, half))

        def step(t, carry, a_re=a_re, a_im=a_im):
            s_re, s_im = carry
            rows = pl.ds(pl.multiple_of(t * n_batch, n_batch), n_batch)
            n_re = a_re * s_re - a_im * s_im + bu_scr[rows, :half]
            n_im = a_re * s_im + a_im * s_re + bu_scr[rows, half:]
            bu_scr[rows, :half] = n_re
            bu_scr[rows, half:] = n_im
            return n_re, n_im

        s_re, s_im = lax.fori_loop(0, chunk, step, (st_re[j], st_im[j]), unroll=True)
        st_re[j] = s_re
        st_im[j] = s_im
        y_scr[:, ch] = jnp.dot(bu_scr[...].astype(BF16), cmat_ref[j], preferred_element_type=F32)

    y = _gelu_tanh(y_scr[...] + dskip_ref[...] * u_scr[...].astype(F32))
    gate = jnp.dot(y.astype(BF16), wglu_ref[...], preferred_element_type=F32) + bglu_ref[...]
    y = (y * jax.nn.sigmoid(gate)).astype(BF16)
    for g in range(n_perm):
        yg = jnp.dot(perm_t_ref[...], y[g * perm_rows:(g + 1) * perm_rows, :], preferred_element_type=F32)
        o_ref[:, g * SSM_PERM_T:(g + 1) * SSM_PERM_T, :] = yg.reshape(
            n_batch, SSM_PERM_T, yg.shape[1]).astype(BF16)


def _ssm_params(lam_re, lam_im, log_dt, b_re, b_im, c_re, c_im):
    groups = lam_re.shape[0]
    per_blk = SSM_CH_BLOCK // SSM_GROUP_CH
    n_blk = groups // per_blk
    dt = jnp.exp(log_dt.astype(F32))[:, None]
    lam_re = lam_re.astype(F32)
    lam_im = lam_im.astype(F32)
    mag = jnp.exp(lam_re * dt)
    bar_re = mag * jnp.cos(lam_im * dt)
    bar_im = mag * jnp.sin(lam_im * dt)
    den = lam_re * lam_re + lam_im * lam_im
    gain_re = ((bar_re - 1.0) * lam_re + bar_im * lam_im) / den
    gain_im = (bar_im * lam_re - (bar_re - 1.0) * lam_im) / den
    bb_re = gain_re[..., None] * b_re - gain_im[..., None] * b_im
    bb_im = gain_re[..., None] * b_im + gain_im[..., None] * b_re
    eye = jnp.eye(per_blk, dtype=F32)

    def in_blocks(m):
        m = m.reshape(n_blk, per_blk, SSM_STATE, SSM_GROUP_CH)
        return jnp.einsum('jgnc,gh->jgchn', m, eye).reshape(n_blk, SSM_CH_BLOCK, per_blk * SSM_STATE)

    def out_blocks(m):
        m = m.reshape(n_blk, per_blk, SSM_GROUP_CH, SSM_STATE)
        return jnp.einsum('jgcn,gh->jhngc', m, eye).reshape(n_blk, per_blk * SSM_STATE, SSM_CH_BLOCK)

    bmat = jnp.concatenate([in_blocks(bb_re), in_blocks(bb_im)], axis=2).astype(BF16)
    cmat = jnp.concatenate([out_blocks(c_re.astype(F32)), out_blocks(-c_im.astype(F32))], axis=1).astype(BF16)
    a_re = bar_re.reshape(n_blk, 1, per_blk * SSM_STATE)
    a_im = bar_im.reshape(n_blk, 1, per_blk * SSM_STATE)
    return bmat, cmat, a_re, a_im


def _time_major_perm(n_batch):
    rows = n_batch * SSM_PERM_T
    perm = np.zeros((rows, rows), np.float32)
    for bi in range(n_batch):
        for tl in range(SSM_PERM_T):
            perm[tl * n_batch + bi, bi * SSM_PERM_T + tl] = 1.0
    return perm


def _ssm_call(x, w_u, bmat, cmat, a_re, a_im, d_skip, w_glu, b_glu):
    b, s, d = x.shape
    width = w_u.shape[1]
    chunk = min(SSM_CHUNK, s)
    rows = b * chunk
    n_blk, _, n_state2 = bmat.shape
    perm = _time_major_perm(b)
    const = lambda *shape: pl.BlockSpec(shape, lambda i: (0,) * len(shape))
    return pl.pallas_call(
        _ssm_kernel,
        grid=(s // chunk,),
        in_specs=[pl.BlockSpec((b, chunk, d), lambda i: (0, i, 0)),
                  const(d, width), const(*perm.shape), const(*perm.shape),
                  const(*bmat.shape), const(*cmat.shape), const(*a_re.shape), const(*a_im.shape),
                  const(1, width), const(width, width), const(1, width)],
        out_specs=pl.BlockSpec((b, chunk, width), lambda i: (0, i, 0)),
        out_shape=jax.ShapeDtypeStruct((b, s, width), BF16),
        scratch_shapes=[pltpu.VMEM((n_blk, b, n_state2 // 2), F32),
                        pltpu.VMEM((n_blk, b, n_state2 // 2), F32),
                        pltpu.VMEM((rows, width), BF16),
                        pltpu.VMEM((rows, n_state2), F32),
                        pltpu.VMEM((rows, width), F32)],
        compiler_params=_params(("arbitrary",)),
        name="s5_mixer",
    )(x, w_u, jnp.asarray(perm, BF16), jnp.asarray(perm.T, BF16), bmat, cmat, a_re, a_im,
      d_skip, w_glu, b_glu)


def _layer_norm(v, g, b):
    mu = jnp.mean(v, axis=-1, keepdims=True)
    c = v - mu
    var = jnp.mean(c * c, axis=-1, keepdims=True)
    return c * lax.rsqrt(var + LN_EPS) * g + b


def _router_gates(scores, sel):
    rows = scores.shape[1]
    per_group = N_EXPERTS // N_EXPERT_GROUPS
    neg_inf = -jnp.inf
    sel3 = sel.reshape(N_EXPERT_GROUPS, per_group, rows)
    sub = lax.broadcasted_iota(jnp.int32, sel3.shape, 1)
    m1 = jnp.max(sel3, axis=1, keepdims=True)
    first = jnp.min(jnp.where(sel3 == m1, sub, per_group), axis=1, keepdims=True)
    m2 = jnp.max(jnp.where(sub == first, neg_inf, sel3), axis=1, keepdims=True)
    grp = jnp.broadcast_to(m1 + m2, sel3.shape)
    kept = []
    for g in range(N_EXPERT_GROUPS):
        beaten = jnp.zeros(grp.shape[1:], F32)
        for o in range(N_EXPERT_GROUPS):
            if o == g:
                continue
            wins = (grp[o] >= grp[g]) if o < g else (grp[o] > grp[g])
            beaten = beaten + jnp.where(wins, 1.0, 0.0)
        kept.append(jnp.where(beaten < TOPK_GROUPS, sel3[g], neg_inf))
    work = jnp.stack(kept, axis=0).reshape(N_EXPERTS, rows)
    eidx = lax.broadcasted_iota(jnp.int32, work.shape, 0)
    w = jnp.zeros(work.shape, F32)
    for _ in range(TOP_K):
        m = jnp.max(work, axis=0, keepdims=True)
        pick = jnp.min(jnp.where(work == m, eidx, N_EXPERTS), axis=0, keepdims=True)
        hit = eidx == pick
        w = jnp.where(hit, scores, w)
        work = jnp.where(hit, neg_inf, work)
    return w / jnp.sum(w, axis=0, keepdims=True) * ROUTED_SCALE


def _post_kernel(alpha, x_ref, attn_ref, ssm_ref, p_ref, wout_ref, g1_ref, b1_ref, wr_hi_ref, wr_lo_ref,
                 rbias_ref, wsg_ref, wsu_ref, wsd_ref, wple_ref, wpg_ref, r_ref, h16_ref, gates_ref):
    n_pairs, _, lanes = attn_ref.shape[1:]
    mix = jnp.dot(ssm_ref[...], wout_ref[n_pairs * lanes:, :], preferred_element_type=F32)
    for pair in range(n_pairs):
        mix = mix + jnp.dot(attn_ref[0, pair], wout_ref[pair * lanes:(pair + 1) * lanes, :],
                            preferred_element_type=F32)
    h = _layer_norm(alpha * x_ref[...] + mix, g1_ref[...], b1_ref[...])
    h16 = h.astype(BF16)
    h16_ref[...] = h16
    h_lo = (h - h16.astype(F32)).astype(BF16)
    logits = _dot_nt(wr_hi_ref[...], h16) + _dot_nt(wr_hi_ref[...], h_lo) + _dot_nt(wr_lo_ref[...], h16)
    scores = jax.nn.sigmoid(logits)
    gates = _router_gates(scores, scores + rbias_ref[...])
    gates = jnp.concatenate([gates, jnp.zeros_like(gates)], axis=0)
    gates_ref[...] = gates.T
    sg = jnp.dot(h16, wsg_ref[...], preferred_element_type=F32)
    su = jnp.dot(h16, wsu_ref[...], preferred_element_type=F32)
    shared = jnp.dot((sg * jax.nn.sigmoid(sg) * su).astype(BF16), wsd_ref[...], preferred_element_type=F32)
    ple = (jnp.dot(p_ref[...].astype(BF16), wple_ref[...], preferred_element_type=F32)
           * jax.nn.sigmoid(jnp.dot(h16, wpg_ref[...], preferred_element_type=F32)))
    r_ref[...] = alpha * h + shared + ple


def _post_call(alpha, x2, attn, ssm2, p2, w_out, g1, b1, wr_hi, wr_lo, rbias, wsg, wsu, wsd, wple, wpg):
    t, d = x2.shape
    _, n_pairs, s, lanes = attn.shape
    rows = min(POST_ROWS, s)
    per_seq = s // rows
    row_blk = lambda w: pl.BlockSpec((rows, w), lambda i: (i, 0))
    const = lambda a: pl.BlockSpec(a.shape, lambda i: (0,) * a.ndim)
    attn_blk = pl.BlockSpec((1, n_pairs, rows, lanes), lambda i: (i // per_seq, 0, i % per_seq, 0))
    weights = (w_out, g1, b1, wr_hi, wr_lo, rbias, wsg, wsu, wsd, wple, wpg)
    return pl.pallas_call(
        functools.partial(_post_kernel, alpha),
        grid=(t // rows,),
        in_specs=[row_blk(d), attn_blk, row_blk(ssm2.shape[1]), row_blk(p2.shape[1])]
                 + [const(w) for w in weights],
        out_specs=[row_blk(d), row_blk(d), row_blk(2 * N_EXPERTS)],
        out_shape=[jax.ShapeDtypeStruct((t, d), F32), jax.ShapeDtypeStruct((t, d), BF16),
                   jax.ShapeDtypeStruct((t, 2 * N_EXPERTS), F32)],
        compiler_params=_params(("arbitrary",)),
        name="post_mix_router",
    )(x2, attn, ssm2, p2, *weights)


def _moe_kernel(h_ref, gates_ref, r_ref, wg_ref, wu_ref, wd_ref, g2_ref, b2_ref, o_ref, acc_ref, hid_ref):
    step = pl.program_id(1)
    n_exp, _, hidden = wg_ref.shape

    @pl.when(step == 0)
    def _():
        acc_ref[...] = jnp.zeros_like(acc_ref)

    lanes = gates_ref.shape[1]
    g = pltpu.roll(gates_ref[...], (lanes - n_exp * step) % lanes, axis=1)
    h = h_ref[...]
    for j in range(n_exp):
        zg = jnp.dot(h, wg_ref[j], preferred_element_type=F32)
        zu = jnp.dot(h, wu_ref[j], preferred_element_type=F32)
        hid = zg * jax.nn.sigmoid(zg) * zu * g[:, j:j + 1]
        hid_ref[:, j * hidden:(j + 1) * hidden] = hid.astype(BF16)
    wd = wd_ref[...].reshape(n_exp * hidden, wd_ref.shape[2])
    acc_ref[...] += jnp.dot(hid_ref[...], wd, preferred_element_type=F32)

    @pl.when(step == pl.num_programs(1) - 1)
    def _():
        o_ref[...] = _layer_norm(r_ref[...] + acc_ref[...], g2_ref[...], b2_ref[...])


def _moe_call(h16, gates, r, wg, wu, wd, g2, b2):
    t, d = h16.shape
    rows = min(MOE_ROWS, t)
    n_exp = MOE_EXPERTS_PER_STEP
    hidden = wg.shape[2]
    row_blk = lambda w: pl.BlockSpec((rows, w), lambda i, e: (i, 0))
    return pl.pallas_call(
        _moe_kernel,
        grid=(t // rows, wg.shape[0] // n_exp),
        in_specs=[row_blk(d), row_blk(gates.shape[1]), row_blk(d),
                  pl.BlockSpec((n_exp, d, hidden), lambda i, e: (e, 0, 0)),
                  pl.BlockSpec((n_exp, d, hidden), lambda i, e: (e, 0, 0)),
                  pl.BlockSpec((n_exp, hidden, d), lambda i, e: (e, 0, 0)),
                  pl.BlockSpec((1, d), lambda i, e: (0, 0)),
                  pl.BlockSpec((1, d), lambda i, e: (0, 0))],
        out_specs=row_blk(d),
        out_shape=jax.ShapeDtypeStruct((t, d), F32),
        scratch_shapes=[pltpu.VMEM((rows, d), F32), pltpu.VMEM((rows, n_exp * hidden), BF16)],
        compiler_params=_params(("arbitrary", "arbitrary")),
        name="routed_experts",
    )(h16, gates, r, wg, wu, wd, g2, b2)


def _layer(h, p_i, w_in, lam_re, lam_im, log_dt, b_re, b_im, c_re, c_im, d_skip, w_glu, b_glu, w_out,
           ln1_g, ln1_b, w_router, router_bias, w_gate, w_up, w_down, ws_gate, ws_up, ws_down, w_ple,
           w_ple_gate, ln2_g, ln2_b, alpha):
    b, s, d = h.shape
    t = b * s
    row = lambda v: v.reshape(1, -1).astype(F32)

    q, k, v = _qkv_call(h, _qkv_weights(w_in), jnp.asarray(_qkv_tables(s), BF16))
    attn = _attn_call(q, k, v, jnp.asarray(_log_multiplicity_table()))

    bmat, cmat, a_re, a_im = _ssm_params(lam_re, lam_im, log_dt, b_re, b_im, c_re, c_im)
    ssm = _ssm_call(h, w_in[:, 3 * ATTN_WIDTH:].astype(BF16), bmat, cmat, a_re, a_im,
                    row(d_skip), w_glu.astype(BF16), row(b_glu))

    wr_t = w_router.astype(F32).T
    wr_hi = wr_t.astype(BF16)
    wr_lo = (wr_t - wr_hi.astype(F32)).astype(BF16)
    r, h16, gates = _post_call(
        alpha, h.reshape(t, d), attn, ssm.reshape(t, -1), p_i.reshape(t, -1),
        w_out.astype(BF16), row(ln1_g), row(ln1_b), wr_hi, wr_lo, router_bias.reshape(-1, 1).astype(F32),
        ws_gate.astype(BF16), ws_up.astype(BF16), ws_down.astype(BF16), w_ple.astype(BF16),
        w_ple_gate.astype(BF16))
    out = _moe_call(h16, gates, r, w_gate.astype(BF16), w_up.astype(BF16), w_down.astype(BF16),
                    row(ln2_g), row(ln2_b))
    return out.reshape(b, s, d)


def kernel(x, p, w_in, lam_re, lam_im, log_dt, b_re, b_im, c_re, c_im, d_skip, w_glu, b_glu, w_out, ln1_g, ln1_b, w_router, router_bias, w_gate, w_up, w_down, ws_gate, ws_up, ws_down, w_ple, w_ple_gate, ln2_g, ln2_b):
    depth = w_in.shape[0]
    alpha = (2.0 * depth) ** 0.25
    h = x
    for i in range(depth):
        h = _layer(h, p[i], w_in[i], lam_re[i], lam_im[i], log_dt[i], b_re[i], b_im[i], c_re[i], c_im[i],
                   d_skip[i], w_glu[i], b_glu[i], w_out[i], ln1_g[i], ln1_b[i], w_router[i], router_bias[i],
                   w_gate[i], w_up[i], w_down[i], ws_gate[i], ws_up[i], ws_down[i], w_ple[i], w_ple_gate[i],
                   ln2_g[i], ln2_b[i], alpha)
    return h
```

```python
import functools
import math

import numpy as np
import jax
import jax.numpy as jnp
from jax import lax
from jax.experimental import pallas as pl
from jax.experimental.pallas import tpu as pltpu

F32 = jnp.float32
BF16 = jnp.bfloat16

HEAD_DIM = 64
N_HEADS = 8
ATTN_WIDTH = N_HEADS * HEAD_DIM
HEAD_LANES = 128
SSM_GROUP_CH = 16
SSM_STATE = 64
DILATED_BRANCHES = ((128, 1), (512, 4), (2048, 16))
N_EXPERTS = 64
TOP_K = 8
N_EXPERT_GROUPS = 8
TOPK_GROUPS = 4
ROUTED_SCALE = 2.5
LN_EPS = 1e-5
MASK_VALUE = -1e30

ATT_BLOCK = 256
ATT_LOOKBACK = max(w for w, _ in DILATED_BRANCHES) // ATT_BLOCK
ATT_WINDOWS = (3, 6, ATT_LOOKBACK + 1)
ATT_PAIRS_PER_ITER = 4
QKV_ROWS = 512
SSM_CHUNK = 128
SSM_PERM_T = 32
SSM_CH_BLOCK = 128
POST_ROWS = 1024
MOE_ROWS = 1024
MOE_EXPERTS_PER_STEP = 4
VMEM_LIMIT = 56 * 1024 * 1024


def _params(sem, vmem=VMEM_LIMIT):
    return pltpu.CompilerParams(dimension_semantics=sem, vmem_limit_bytes=vmem)


def _qkv_kernel(x_ref, w_ref, pos_ref, q_ref, k_ref, v_ref):
    z = jnp.dot(x_ref[0].astype(BF16), w_ref[...], preferred_element_type=F32).astype(BF16)
    lane = lax.broadcasted_iota(jnp.int32, (z.shape[0], HEAD_LANES), 1)
    for c, o_ref in enumerate((q_ref, k_ref, v_ref)):
        for h in range(N_HEADS):
            zc = z[:, c * ATTN_WIDTH + (h // 2) * HEAD_LANES:c * ATTN_WIDTH + (h // 2 + 1) * HEAD_LANES]
            pos = pos_ref[:, (c * N_HEADS + h) * HEAD_LANES:(c * N_HEADS + h + 1) * HEAD_LANES]
            data = (lane < HEAD_DIM) if h % 2 == 0 else (lane >= HEAD_DIM)
            o_ref[0, h] = jnp.where(data, zc, pos)


def _qkv_tables(seq):
    t = np.arange(seq)
    width = N_HEADS * HEAD_LANES
    qpos = np.zeros((seq, width), np.float32)
    kpos = np.zeros((seq, width), np.float32)
    vpos = np.zeros((seq, width), np.float32)
    for h in range(N_HEADS):
        slope = 2.0 ** (-8.0 * (h + 1) / N_HEADS)
        base = h * HEAD_LANES + (HEAD_DIM if h % 2 == 0 else 0)
        qpos[:, base + 0] = -slope * (t % ATT_BLOCK)
        qpos[:, base + 1] = -slope * ATT_BLOCK * (t // ATT_BLOCK)
        qpos[:, base + 2] = slope
        qpos[:, base + 3] = slope * ATT_BLOCK
        kpos[:, base + 0] = 1.0
        kpos[:, base + 1] = 1.0
        kpos[:, base + 2] = t % ATT_BLOCK
        kpos[:, base + 3] = t // ATT_BLOCK
        ones_lane = h * HEAD_LANES + (HEAD_DIM if h % 2 == 0 else 0)
        vpos[:, ones_lane] = 1.0
    return np.concatenate([qpos, kpos, vpos], axis=1)


def _qkv_weights(w_in):
    wq = w_in[:, :ATTN_WIDTH] * (HEAD_DIM ** -0.5)
    return jnp.concatenate([wq, w_in[:, ATTN_WIDTH:3 * ATTN_WIDTH]], axis=1).astype(BF16)


def _qkv_call(x, w_qkv, pos):
    b, s, d = x.shape
    width = N_HEADS * HEAD_LANES
    rows = min(QKV_ROWS, s)
    out = jax.ShapeDtypeStruct((b, N_HEADS, s, HEAD_LANES), BF16)
    blk = pl.BlockSpec((1, N_HEADS, rows, HEAD_LANES), lambda si, bi: (bi, 0, si, 0))
    return pl.pallas_call(
        _qkv_kernel,
        grid=(s // rows, b),
        in_specs=[pl.BlockSpec((1, rows, d), lambda si, bi: (bi, si, 0)),
                  pl.BlockSpec(w_qkv.shape, lambda si, bi: (0, 0)),
                  pl.BlockSpec((rows, 3 * width), lambda si, bi: (si, 0))],
        out_specs=[blk, blk, blk],
        out_shape=[out, out, out],
        compiler_params=_params(("arbitrary", "arbitrary")),
        name="qkv_proj",
    )(x, w_qkv, pos)


def _log_multiplicity_table():
    a = (ATT_LOOKBACK - np.arange(2 * ATT_LOOKBACK + 1))[:, None, None]
    i = np.arange(ATT_BLOCK)[None, :, None]
    j = np.arange(ATT_BLOCK)[None, None, :]
    dist = ATT_BLOCK * a + i - j
    mult = np.zeros(dist.shape, np.float64)
    for window, dil in DILATED_BRANCHES:
        mult += (dist >= 0) & (dist <= window) & (dist % dil == 0)
    return np.where(mult > 0, np.log(np.maximum(mult, 1.0)), MASK_VALUE).astype(np.float32)


def _dot_nt(a, b):
    return lax.dot_general(a, b, (((1,), (1,)), ((), ())), preferred_element_type=F32)


def _attn_kernel(windows, q_ref, k_ref, v_ref, bias_ref, o_ref):
    qi = pl.program_id(1)
    lo = 0
    for idx, n_win in enumerate(windows):
        last = idx == len(windows) - 1
        cond = (qi >= lo) if last else jnp.logical_and(qi >= lo, qi < n_win)
        pl.when(cond)(functools.partial(_attn_window, n_win, q_ref, k_ref, v_ref, bias_ref, o_ref))
        lo = n_win


def _attn_window(n_win, q_ref, k_ref, v_ref, bias_ref, o_ref):
    qi = pl.program_id(1)
    first = jnp.maximum(qi - (n_win - 1), 0)
    start = pl.multiple_of(first * ATT_BLOCK, ATT_BLOCK)
    u0 = ATT_LOOKBACK - (qi - first)
    lane = lax.broadcasted_iota(jnp.int32, (ATT_BLOCK, HEAD_LANES), 1)

    def pairs_body(it, carry):
        for sub in range(ATT_PAIRS_PER_ITER):
            pair = it * ATT_PAIRS_PER_ITER + sub
            outs = []
            for e in range(2):
                h = 2 * pair + e
                kw = k_ref[0, h, pl.ds(start, n_win * ATT_BLOCK), :]
                vw = v_ref[0, h, pl.ds(start, n_win * ATT_BLOCK), :]
                s = _dot_nt(q_ref[0, h], kw)
                s = s + jnp.concatenate([bias_ref[u0 + c] for c in range(n_win)], axis=1)
                p = jnp.exp(s - jnp.max(s, axis=1, keepdims=True))
                acc = jnp.dot(p.astype(BF16), vw, preferred_element_type=F32)
                ones_lane = HEAD_DIM if e == 0 else 0
                outs.append(acc * (1.0 / acc[:, ones_lane:ones_lane + 1]))
            o_ref[0, pair] = jnp.where(lane < HEAD_DIM, outs[0], outs[1]).astype(BF16)
        return carry

    lax.fori_loop(0, N_HEADS // (2 * ATT_PAIRS_PER_ITER), pairs_body, 0)


def _attn_call(q, k, v, bias):
    b, n_heads, s, lanes = q.shape
    n_win = min(ATT_LOOKBACK + 1, s // ATT_BLOCK)
    windows = tuple(sorted({min(w, n_win) for w in ATT_WINDOWS}))
    once = pl.Buffered(1)
    return pl.pallas_call(
        functools.partial(_attn_kernel, windows),
        grid=(b, s // ATT_BLOCK),
        in_specs=[pl.BlockSpec((1, n_heads, ATT_BLOCK, lanes), lambda bi, qi: (bi, 0, qi, 0)),
                  pl.BlockSpec((1, n_heads, s, lanes), lambda bi, qi: (bi, 0, 0, 0), pipeline_mode=once),
                  pl.BlockSpec((1, n_heads, s, lanes), lambda bi, qi: (bi, 0, 0, 0), pipeline_mode=once),
                  pl.BlockSpec(bias.shape, lambda bi, qi: (0, 0, 0), pipeline_mode=once)],
        out_specs=pl.BlockSpec((1, n_heads // 2, ATT_BLOCK, lanes), lambda bi, qi: (bi, 0, qi, 0)),
        out_shape=jax.ShapeDtypeStruct((b, n_heads // 2, s, lanes), BF16),
        compiler_params=_params(("arbitrary", "arbitrary")),
        name="banded_attention",
    )(q, k, v, bias)


def _gelu_tanh(y):
    return 0.5 * y * (1.0 + jnp.tanh(math.sqrt(2.0 / math.pi) * (y + 0.044715 * (y * y * y))))


def _ssm_kernel(x_ref, wu_ref, perm_ref, perm_t_ref, bmat_ref, cmat_ref, a_re_ref, a_im_ref, dskip_ref,
                wglu_ref, bglu_ref, o_ref, st_re, st_im, u_scr, bu_scr, y_scr):
    n_batch, chunk, _ = x_ref.shape
    n_perm = chunk // SSM_PERM_T
    perm_rows = n_batch * SSM_PERM_T
    n_blocks = bmat_ref.shape[0]
    half = bmat_ref.shape[2] // 2

    @pl.when(pl.program_id(0) == 0)
    def _():
        st_re[...] = jnp.zeros_like(st_re)
        st_im[...] = jnp.zeros_like(st_im)

    for g in range(n_perm):
        xg = x_ref[:, g * SSM_PERM_T:(g + 1) * SSM_PERM_T, :].reshape(perm_rows, x_ref.shape[2])
        ug = jnp.dot(xg.astype(BF16), wu_ref[...], preferred_element_type=F32).astype(BF16)
        u_scr[g * perm_rows:(g + 1) * perm_rows, :] = jnp.dot(
            perm_ref[...], ug, preferred_element_type=F32).astype(BF16)

    for j in range(n_blocks):
        ch = slice(j * SSM_CH_BLOCK, (j + 1) * SSM_CH_BLOCK)
        bu_scr[...] = jnp.dot(u_scr[:, ch], bmat_ref[j], preferred_element_type=F32)
        a_re = jnp.broadcast_to(a_re_ref[j], (n_batch, half))
        a_im = jnp.broadcast_to(a_im_ref[j], (n_batch, half))

        def step(t, carry, a_re=a_re, a_im=a_im):
            s_re, s_im = carry
            rows = pl.ds(pl.multiple_of(t * n_batch, n_batch), n_batch)
            n_re = a_re * s_re - a_im * s_im + bu_scr[rows, :half]
            n_im = a_re * s_im + a_im * s_re + bu_scr[rows, half:]
            bu_scr[rows, :half] = n_re
            bu_scr[rows, half:] = n_im
            return n_re, n_im

        s_re, s_im = lax.fori_loop(0, chunk, step, (st_re[j], st_im[j]), unroll=True)
        st_re[j] = s_re
        st_im[j] = s_im
        y_scr[:, ch] = jnp.dot(bu_scr[...].astype(BF16), cmat_ref[j], preferred_element_type=F32)

    y = _gelu_tanh(y_scr[...] + dskip_ref[...] * u_scr[...].astype(F32))
    gate = jnp.dot(y.astype(BF16), wglu_ref[...], preferred_element_type=F32) + bglu_ref[...]
    y = (y * jax.nn.sigmoid(gate)).astype(BF16)
    for g in range(n_perm):
        yg = jnp.dot(perm_t_ref[...], y[g * perm_rows:(g + 1) * perm_rows, :], preferred_element_type=F32)
        o_ref[:, g * SSM_PERM_T:(g + 1) * SSM_PERM_T, :] = yg.reshape(
            n_batch, SSM_PERM_T, yg.shape[1]).astype(BF16)


def _ssm_params(lam_re, lam_im, log_dt, b_re, b_im, c_re, c_im):
    groups = lam_re.shape[0]
    per_blk = SSM_CH_BLOCK // SSM_GROUP_CH
    n_blk = groups // per_blk
    dt = jnp.exp(log_dt.astype(F32))[:, None]
    lam_re = lam_re.astype(F32)
    lam_im = lam_im.astype(F32)
    mag = jnp.exp(lam_re * dt)
    bar_re = mag * jnp.cos(lam_im * dt)
    bar_im = mag * jnp.sin(lam_im * dt)
    den = lam_re * lam_re + lam_im * lam_im
    gain_re = ((bar_re - 1.0) * lam_re + bar_im * lam_im) / den
    gain_im = (bar_im * lam_re - (bar_re - 1.0) * lam_im) / den
    bb_re = gain_re[..., None] * b_re - gain_im[..., None] * b_im
    bb_im = gain_re[..., None] * b_im + gain_im[..., None] * b_re
    eye = jnp.eye(per_blk, dtype=F32)

    def in_blocks(m):
        m = m.reshape(n_blk, per_blk, SSM_STATE, SSM_GROUP_CH)
        return jnp.einsum('jgnc,gh->jgchn', m, eye).reshape(n_blk, SSM_CH_BLOCK, per_blk * SSM_STATE)

    def out_blocks(m):
        m = m.reshape(n_blk, per_blk, SSM_GROUP_CH, SSM_STATE)
        return jnp.einsum('jgcn,gh->jhngc', m, eye).reshape(n_blk, per_blk * SSM_STATE, SSM_CH_BLOCK)

    bmat = jnp.concatenate([in_blocks(bb_re), in_blocks(bb_im)], axis=2).astype(BF16)
    cmat = jnp.concatenate([out_blocks(c_re.astype(F32)), out_blocks(-c_im.astype(F32))], axis=1).astype(BF16)
    a_re = bar_re.reshape(n_blk, 1, per_blk * SSM_STATE)
    a_im = bar_im.reshape(n_blk, 1, per_blk * SSM_STATE)
    return bmat, cmat, a_re, a_im


def _time_major_perm(n_batch):
    rows = n_batch * SSM_PERM_T
    perm = np.zeros((rows, rows), np.float32)
    for bi in range(n_batch):
        for tl in range(SSM_PERM_T):
            perm[tl * n_batch + bi, bi * SSM_PERM_T + tl] = 1.0
    return perm


def _ssm_call(x, w_u, bmat, cmat, a_re, a_im, d_skip, w_glu, b_glu):
    b, s, d = x.shape
    width = w_u.shape[1]
    chunk = min(SSM_CHUNK, s)
    rows = b * chunk
    n_blk, _, n_state2 = bmat.shape
    perm = _time_major_perm(b)
    const = lambda *shape: pl.BlockSpec(shape, lambda i: (0,) * len(shape))
    return pl.pallas_call(
        _ssm_kernel,
        grid=(s // chunk,),
        in_specs=[pl.BlockSpec((b, chunk, d), lambda i: (0, i, 0)),
                  const(d, width), const(*perm.shape), const(*perm.shape),
                  const(*bmat.shape), const(*cmat.shape), const(*a_re.shape), const(*a_im.shape),
                  const(1, width), const(width, width), const(1, width)],
        out_specs=pl.BlockSpec((b, chunk, width), lambda i: (0, i, 0)),
        out_shape=jax.ShapeDtypeStruct((b, s, width), BF16),
        scratch_shapes=[pltpu.VMEM((n_blk, b, n_state2 // 2), F32),
                        pltpu.VMEM((n_blk, b, n_state2 // 2), F32),
                        pltpu.VMEM((rows, width), BF16),
                        pltpu.VMEM((rows, n_state2), F32),
                        pltpu.VMEM((rows, width), F32)],
        compiler_params=_params(("arbitrary",)),
        name="s5_mixer",
    )(x, w_u, jnp.asarray(perm, BF16), jnp.asarray(perm.T, BF16), bmat, cmat, a_re, a_im,
      d_skip, w_glu, b_glu)


def _layer_norm(v, g, b):
    mu = jnp.mean(v, axis=-1, keepdims=True)
    c = v - mu
    var = jnp.mean(c * c, axis=-1, keepdims=True)
    return c * lax.rsqrt(var + LN_EPS) * g + b


def _router_gates(scores, sel):
    rows = scores.shape[1]
    per_group = N_EXPERTS // N_EXPERT_GROUPS
    neg_inf = -jnp.inf
    sel3 = sel.reshape(N_EXPERT_GROUPS, per_group, rows)
    sub = lax.broadcasted_iota(jnp.int32, sel3.shape, 1)
    m1 = jnp.max(sel3, axis=1, keepdims=True)
    first = jnp.min(jnp.where(sel3 == m1, sub, per_group), axis=1, keepdims=True)
    m2 = jnp.max(jnp.where(sub == first, neg_inf, sel3), axis=1, keepdims=True)
    grp = jnp.broadcast_to(m1 + m2, sel3.shape)
    kept = []
    for g in range(N_EXPERT_GROUPS):
        beaten = jnp.zeros(grp.shape[1:], F32)
        for o in range(N_EXPERT_GROUPS):
            if o == g:
                continue
            wins = (grp[o] >= grp[g]) if o < g else (grp[o] > grp[g])
            beaten = beaten + jnp.where(wins, 1.0, 0.0)
        kept.append(jnp.where(beaten < TOPK_GROUPS, sel3[g], neg_inf))
    work = jnp.stack(kept, axis=0).reshape(N_EXPERTS, rows)
    eidx = lax.broadcasted_iota(jnp.int32, work.shape, 0)
    w = jnp.zeros(work.shape, F32)
    for _ in range(TOP_K):
        m = jnp.max(work, axis=0, keepdims=True)
        pick = jnp.min(jnp.where(work == m, eidx, N_EXPERTS), axis=0, keepdims=True)
        hit = eidx == pick
        w = jnp.where(hit, scores, w)
        work = jnp.where(hit, neg_inf, work)
    return w / jnp.sum(w, axis=0, keepdims=True) * ROUTED_SCALE


def _post_kernel(alpha, x_ref, attn_ref, ssm_ref, p_ref, wout_ref, g1_ref, b1_ref, wr_hi_ref, wr_lo_ref,
                 rbias_ref, wsg_ref, wsu_ref, wsd_ref, wple_ref, wpg_ref, r_ref, h16_ref, gates_ref):
    n_pairs, _, lanes = attn_ref.shape[1:]
    mix = jnp.dot(ssm_ref[...], wout_ref[n_pairs * lanes:, :], preferred_element_type=F32)
    for pair in range(0, n_pairs, 2):
        a2 = jnp.concatenate([attn_ref[0, pair], attn_ref[0, pair + 1]], axis=1)
        mix = mix + jnp.dot(a2, wout_ref[pair * lanes:(pair + 2) * lanes, :], preferred_element_type=F32)
    h = _layer_norm(alpha * x_ref[...] + mix, g1_ref[...], b1_ref[...])
    h16 = h.astype(BF16)
    h16_ref[...] = h16
    h_lo = (h - h16.astype(F32)).astype(BF16)
    logits = _dot_nt(wr_hi_ref[...], h16) + _dot_nt(wr_hi_ref[...], h_lo) + _dot_nt(wr_lo_ref[...], h16)
    scores = jax.nn.sigmoid(logits)
    gates = _router_gates(scores, scores + rbias_ref[...])
    gates = jnp.concatenate([gates, jnp.zeros_like(gates)], axis=0)
    gates_ref[...] = gates.T
    sg = jnp.dot(h16, wsg_ref[...], preferred_element_type=F32)
    su = jnp.dot(h16, wsu_ref[...], preferred_element_type=F32)
    shared = jnp.dot((sg * jax.nn.sigmoid(sg) * su).astype(BF16), wsd_ref[...], preferred_element_type=F32)
    ple = (jnp.dot(p_ref[...].astype(BF16), wple_ref[...], preferred_element_type=F32)
           * jax.nn.sigmoid(jnp.dot(h16, wpg_ref[...], preferred_element_type=F32)))
    r_ref[...] = alpha * h + shared + ple


def _post_call(alpha, x2, attn, ssm2, p2, w_out, g1, b1, wr_hi, wr_lo, rbias, wsg, wsu, wsd, wple, wpg):
    t, d = x2.shape
    _, n_pairs, s, lanes = attn.shape
    rows = min(POST_ROWS, s)
    per_seq = s // rows
    row_blk = lambda w: pl.BlockSpec((rows, w), lambda i: (i, 0))
    const = lambda a: pl.BlockSpec(a.shape, lambda i: (0,) * a.ndim)
    attn_blk = pl.BlockSpec((1, n_pairs, rows, lanes), lambda i: (i // per_seq, 0, i % per_seq, 0))
    weights = (w_out, g1, b1, wr_hi, wr_lo, rbias, wsg, wsu, wsd, wple, wpg)
    return pl.pallas_call(
        functools.partial(_post_kernel, alpha),
        grid=(t // rows,),
        in_specs=[row_blk(d), attn_blk, row_blk(ssm2.shape[1]), row_blk(p2.shape[1])]
                 + [const(w) for w in weights],
        out_specs=[row_blk(d), row_blk(d), row_blk(2 * N_EXPERTS)],
        out_shape=[jax.ShapeDtypeStruct((t, d), F32), jax.ShapeDtypeStruct((t, d), BF16),
                   jax.ShapeDtypeStruct((t, 2 * N_EXPERTS), F32)],
        compiler_params=_params(("arbitrary",)),
        name="post_mix_router",
    )(x2, attn, ssm2, p2, *weights)


def _moe_kernel(h_ref, gates_ref, r_ref, wg_ref, wu_ref, wd_ref, g2_ref, b2_ref, o_ref, acc_ref, hid_ref):
    step = pl.program_id(1)
    n_exp, _, hidden = wg_ref.shape

    @pl.when(step == 0)
    def _():
        acc_ref[...] = jnp.zeros_like(acc_ref)

    lanes = gates_ref.shape[1]
    g = pltpu.roll(gates_ref[...], (lanes - n_exp * step) % lanes, axis=1)
    h = h_ref[...]
    for j in range(n_exp):
        zg = jnp.dot(h, wg_ref[j], preferred_element_type=F32)
        zu = jnp.dot(h, wu_ref[j], preferred_element_type=F32)
        hid = zg * jax.nn.sigmoid(zg) * zu * g[:, j:j + 1]
        hid_ref[:, j * hidden:(j + 1) * hidden] = hid.astype(BF16)
    wd = wd_ref[...].reshape(n_exp * hidden, wd_ref.shape[2])
    acc_ref[...] += jnp.dot(hid_ref[...], wd, preferred_element_type=F32)

    @pl.when(step == pl.num_programs(1) - 1)
    def _():
        o_ref[...] = _layer_norm(r_ref[...] + acc_ref[...], g2_ref[...], b2_ref[...])


def _moe_call(h16, gates, r, wg, wu, wd, g2, b2):
    t, d = h16.shape
    rows = min(MOE_ROWS, t)
    n_exp = MOE_EXPERTS_PER_STEP
    hidden = wg.shape[2]
    row_blk = lambda w: pl.BlockSpec((rows, w), lambda i, e: (i, 0))
    return pl.pallas_call(
        _moe_kernel,
        grid=(t // rows, wg.shape[0] // n_exp),
        in_specs=[row_blk(d), row_blk(gates.shape[1]), row_blk(d),
                  pl.BlockSpec((n_exp, d, hidden), lambda i, e: (e, 0, 0)),
                  pl.BlockSpec((n_exp, d, hidden), lambda i, e: (e, 0, 0)),
                  pl.BlockSpec((n_exp, hidden, d), lambda i, e: (e, 0, 0)),
                  pl.BlockSpec((1, d), lambda i, e: (0, 0)),
                  pl.BlockSpec((1, d), lambda i, e: (0, 0))],
        out_specs=row_blk(d),
        out_shape=jax.ShapeDtypeStruct((t, d), F32),
        scratch_shapes=[pltpu.VMEM((rows, d), F32), pltpu.VMEM((rows, n_exp * hidden), BF16)],
        compiler_params=_params(("arbitrary", "arbitrary")),
        name="routed_experts",
    )(h16, gates, r, wg, wu, wd, g2, b2)


def _layer(h, p_i, w_in, lam_re, lam_im, log_dt, b_re, b_im, c_re, c_im, d_skip, w_glu, b_glu, w_out,
           ln1_g, ln1_b, w_router, router_bias, w_gate, w_up, w_down, ws_gate, ws_up, ws_down, w_ple,
           w_ple_gate, ln2_g, ln2_b, alpha):
    b, s, d = h.shape
    t = b * s
    row = lambda v: v.reshape(1, -1).astype(F32)

    q, k, v = _qkv_call(h, _qkv_weights(w_in), jnp.asarray(_qkv_tables(s), BF16))
    attn = _attn_call(q, k, v, jnp.asarray(_log_multiplicity_table()))

    bmat, cmat, a_re, a_im = _ssm_params(lam_re, lam_im, log_dt, b_re, b_im, c_re, c_im)
    ssm = _ssm_call(h, w_in[:, 3 * ATTN_WIDTH:].astype(BF16), bmat, cmat, a_re, a_im,
                    row(d_skip), w_glu.astype(BF16), row(b_glu))

    wr_t = w_router.astype(F32).T
    wr_hi = wr_t.astype(BF16)
    wr_lo = (wr_t - wr_hi.astype(F32)).astype(BF16)
    r, h16, gates = _post_call(
        alpha, h.reshape(t, d), attn, ssm.reshape(t, -1), p_i.reshape(t, -1),
        w_out.astype(BF16), row(ln1_g), row(ln1_b), wr_hi, wr_lo, router_bias.reshape(-1, 1).astype(F32),
        ws_gate.astype(BF16), ws_up.astype(BF16), ws_down.astype(BF16), w_ple.astype(BF16),
        w_ple_gate.astype(BF16))
    out = _moe_call(h16, gates, r, w_gate.astype(BF16), w_up.astype(BF16), w_down.astype(BF16),
                    row(ln2_g), row(ln2_b))
    return out.reshape(b, s, d)


def kernel(x, p, w_in, lam_re, lam_im, log_dt, b_re, b_im, c_re, c_im, d_skip, w_glu, b_glu, w_out, ln1_g, ln1_b, w_router, router_bias, w_gate, w_up, w_down, ws_gate, ws_up, ws_down, w_ple, w_ple_gate, ln2_g, ln2_b):
    depth = w_in.shape[0]
    alpha = (2.0 * depth) ** 0.25
    h = x
    for i in range(depth):
        h = _layer(h, p[i], w_in[i], lam_re[i], lam_im[i], log_dt[i], b_re[i], b_im[i], c_re[i], c_im[i],
                   d_skip[i], w_glu[i], b_glu[i], w_out[i], ln1_g[i], ln1_b[i], w_router[i], router_bias[i],
                   w_gate[i], w_up[i], w_down[i], ws_gate[i], ws_up[i], ws_down[i], w_ple[i], w_ple_gate[i],
                   ln2_g[i], ln2_b[i], alpha)
    return h
```

```python
import functools
import math

import numpy as np
import jax
import jax.numpy as jnp
from jax import lax
from jax.experimental import pallas as pl
from jax.experimental.pallas import tpu as pltpu

F32 = jnp.float32
BF16 = jnp.bfloat16

HEAD_DIM = 64
N_HEADS = 8
ATTN_WIDTH = N_HEADS * HEAD_DIM
HEAD_LANES = 128
SSM_GROUP_CH = 16
SSM_STATE = 64
DILATED_BRANCHES = ((128, 1), (512, 4), (2048, 16))
N_EXPERTS = 64
TOP_K = 8
N_EXPERT_GROUPS = 8
TOPK_GROUPS = 4
ROUTED_SCALE = 2.5
LN_EPS = 1e-5
MASK_VALUE = -1e30

ATT_BLOCK = 256
ATT_LOOKBACK = max(w for w, _ in DILATED_BRANCHES) // ATT_BLOCK
ATT_WINDOWS = (3, 6, ATT_LOOKBACK + 1)
QKV_ROWS = 512
SSM_CHUNK = 128
SSM_PERM_T = 32
SSM_CH_BLOCK = 128
POST_ROWS = 1024
MOE_ROWS = 1024
MOE_EXPERTS_PER_STEP = 4
VMEM_LIMIT = 56 * 1024 * 1024


def _params(sem, vmem=VMEM_LIMIT):
    return pltpu.CompilerParams(dimension_semantics=sem, vmem_limit_bytes=vmem)


def _qkv_kernel(x_ref, w_ref, pos_ref, q_ref, k_ref, v_ref):
    z = jnp.dot(x_ref[0].astype(BF16), w_ref[...], preferred_element_type=F32).astype(BF16)
    lane = lax.broadcasted_iota(jnp.int32, (z.shape[0], HEAD_LANES), 1)
    for c, o_ref in enumerate((q_ref, k_ref, v_ref)):
        for h in range(N_HEADS):
            zc = z[:, c * ATTN_WIDTH + (h // 2) * HEAD_LANES:c * ATTN_WIDTH + (h // 2 + 1) * HEAD_LANES]
            pos = pos_ref[:, (c * N_HEADS + h) * HEAD_LANES:(c * N_HEADS + h + 1) * HEAD_LANES]
            data = (lane < HEAD_DIM) if h % 2 == 0 else (lane >= HEAD_DIM)
            o_ref[0, h] = jnp.where(data, zc, pos)


def _qkv_tables(seq):
    t = np.arange(seq)
    width = N_HEADS * HEAD_LANES
    qpos = np.zeros((seq, width), np.float32)
    kpos = np.zeros((seq, width), np.float32)
    vpos = np.zeros((seq, width), np.float32)
    for h in range(N_HEADS):
        slope = 2.0 ** (-8.0 * (h + 1) / N_HEADS)
        base = h * HEAD_LANES + (HEAD_DIM if h % 2 == 0 else 0)
        qpos[:, base + 0] = -slope * (t % ATT_BLOCK)
        qpos[:, base + 1] = -slope * ATT_BLOCK * (t // ATT_BLOCK)
        qpos[:, base + 2] = slope
        qpos[:, base + 3] = slope * ATT_BLOCK
        kpos[:, base + 0] = 1.0
        kpos[:, base + 1] = 1.0
        kpos[:, base + 2] = t % ATT_BLOCK
        kpos[:, base + 3] = t // ATT_BLOCK
        ones_lane = h * HEAD_LANES + (HEAD_DIM if h % 2 == 0 else 0)
        vpos[:, ones_lane] = 1.0
    return np.concatenate([qpos, kpos, vpos], axis=1)


def _qkv_weights(w_in):
    wq = w_in[:, :ATTN_WIDTH] * (HEAD_DIM ** -0.5)
    return jnp.concatenate([wq, w_in[:, ATTN_WIDTH:3 * ATTN_WIDTH]], axis=1).astype(BF16)


def _qkv_call(x, w_qkv, pos):
    b, s, d = x.shape
    width = N_HEADS * HEAD_LANES
    rows = min(QKV_ROWS, s)
    out = jax.ShapeDtypeStruct((b, N_HEADS, s, HEAD_LANES), BF16)
    blk = pl.BlockSpec((1, N_HEADS, rows, HEAD_LANES), lambda si, bi: (bi, 0, si, 0))
    return pl.pallas_call(
        _qkv_kernel,
        grid=(s // rows, b),
        in_specs=[pl.BlockSpec((1, rows, d), lambda si, bi: (bi, si, 0)),
                  pl.BlockSpec(w_qkv.shape, lambda si, bi: (0, 0)),
                  pl.BlockSpec((rows, 3 * width), lambda si, bi: (si, 0))],
        out_specs=[blk, blk, blk],
        out_shape=[out, out, out],
        compiler_params=_params(("arbitrary", "arbitrary")),
        name="qkv_proj",
    )(x, w_qkv, pos)


def _log_multiplicity_table():
    a = (ATT_LOOKBACK - np.arange(2 * ATT_LOOKBACK + 1))[:, None, None]
    i = np.arange(ATT_BLOCK)[None, :, None]
    j = np.arange(ATT_BLOCK)[None, None, :]
    dist = ATT_BLOCK * a + i - j
    mult = np.zeros(dist.shape, np.float64)
    for window, dil in DILATED_BRANCHES:
        mult += (dist >= 0) & (dist <= window) & (dist % dil == 0)
    return np.where(mult > 0, np.log(np.maximum(mult, 1.0)), MASK_VALUE).astype(np.float32)


def _dot_nt(a, b):
    return lax.dot_general(a, b, (((1,), (1,)), ((), ())), preferred_element_type=F32)


def _attn_kernel(windows, q_ref, k_ref, v_ref, bias_ref, o_ref):
    qi = pl.program_id(1)
    lo = 0
    for idx, n_win in enumerate(windows):
        last = idx == len(windows) - 1
        cond = (qi >= lo) if last else jnp.logical_and(qi >= lo, qi < n_win)
        pl.when(cond)(functools.partial(_attn_window, n_win, q_ref, k_ref, v_ref, bias_ref, o_ref))
        lo = n_win


def _attn_window(n_win, q_ref, k_ref, v_ref, bias_ref, o_ref):
    qi = pl.program_id(1)
    first = jnp.maximum(qi - (n_win - 1), 0)
    start = pl.multiple_of(first * ATT_BLOCK, ATT_BLOCK)
    u0 = ATT_LOOKBACK - (qi - first)
    lane = lax.broadcasted_iota(jnp.int32, (ATT_BLOCK, HEAD_LANES), 1)

    window = pl.ds(start, n_win * ATT_BLOCK)

    def scores(h):
        s = _dot_nt(q_ref[0, h], k_ref[0, h, window, :])
        return s + jnp.concatenate([bias_ref[u0 + c] for c in range(n_win)], axis=1)

    def finish(h, s):
        p = jnp.exp(s - jnp.max(s, axis=1, keepdims=True))
        acc = jnp.dot(p.astype(BF16), v_ref[0, h, window, :], preferred_element_type=F32)
        ones_lane = HEAD_DIM if h % 2 == 0 else 0
        return acc * (1.0 / acc[:, ones_lane:ones_lane + 1])

    s_next = scores(0)
    outs = []
    for h in range(N_HEADS):
        s_cur = s_next
        if h + 1 < N_HEADS:
            s_next = scores(h + 1)
        outs.append(finish(h, s_cur))
        if h % 2 == 1:
            o_ref[0, h // 2] = jnp.where(lane < HEAD_DIM, outs[h - 1], outs[h]).astype(BF16)


def _attn_call(q, k, v, bias):
    b, n_heads, s, lanes = q.shape
    n_win = min(ATT_LOOKBACK + 1, s // ATT_BLOCK)
    windows = tuple(sorted({min(w, n_win) for w in ATT_WINDOWS}))
    once = pl.Buffered(1)
    return pl.pallas_call(
        functools.partial(_attn_kernel, windows),
        grid=(b, s // ATT_BLOCK),
        in_specs=[pl.BlockSpec((1, n_heads, ATT_BLOCK, lanes), lambda bi, qi: (bi, 0, qi, 0)),
                  pl.BlockSpec((1, n_heads, s, lanes), lambda bi, qi: (bi, 0, 0, 0), pipeline_mode=once),
                  pl.BlockSpec((1, n_heads, s, lanes), lambda bi, qi: (bi, 0, 0, 0), pipeline_mode=once),
                  pl.BlockSpec(bias.shape, lambda bi, qi: (0, 0, 0), pipeline_mode=once)],
        out_specs=pl.BlockSpec((1, n_heads // 2, ATT_BLOCK, lanes), lambda bi, qi: (bi, 0, qi, 0)),
        out_shape=jax.ShapeDtypeStruct((b, n_heads // 2, s, lanes), BF16),
        compiler_params=_params(("arbitrary", "arbitrary")),
        name="banded_attention",
    )(q, k, v, bias)


def _gelu_tanh(y):
    return 0.5 * y * (1.0 + jnp.tanh(math.sqrt(2.0 / math.pi) * (y + 0.044715 * (y * y * y))))


def _ssm_kernel(x_ref, wu_ref, perm_ref, perm_t_ref, bmat_ref, cmat_ref, a_re_ref, a_im_ref, dskip_ref,
                wglu_ref, bglu_ref, o_ref, st_re, st_im, u_scr, bu_scr, y_scr):
    n_batch, chunk, _ = x_ref.shape
    n_perm = chunk // SSM_PERM_T
    perm_rows = n_batch * SSM_PERM_T
    n_blocks = bmat_ref.shape[0]
    half = bmat_ref.shape[2] // 2

    @pl.when(pl.program_id(0) == 0)
    def _():
        st_re[...] = jnp.zeros_like(st_re)
        st_im[...] = jnp.zeros_like(st_im)

    for g in range(n_perm):
        xg = x_ref[:, g * SSM_PERM_T:(g + 1) * SSM_PERM_T, :].reshape(perm_rows, x_ref.shape[2])
        ug = jnp.dot(xg.astype(BF16), wu_ref[...], preferred_element_type=F32).astype(BF16)
        u_scr[g * perm_rows:(g + 1) * perm_rows, :] = jnp.dot(
            perm_ref[...], ug, preferred_element_type=F32).astype(BF16)

    def drive(j):
        ch = slice(j * SSM_CH_BLOCK, (j + 1) * SSM_CH_BLOCK)
        bu_scr[j % 2] = jnp.dot(u_scr[:, ch], bmat_ref[j], preferred_element_type=F32)

    drive(0)
    for j in range(n_blocks):
        ch = slice(j * SSM_CH_BLOCK, (j + 1) * SSM_CH_BLOCK)
        if j + 1 < n_blocks:
            drive(j + 1)
        buf = bu_scr.at[j % 2]
        a_re = jnp.broadcast_to(a_re_ref[j], (n_batch, half))
        a_im = jnp.broadcast_to(a_im_ref[j], (n_batch, half))

        def step(t, carry, a_re=a_re, a_im=a_im, buf=buf):
            s_re, s_im = carry
            rows = pl.ds(pl.multiple_of(t * n_batch, n_batch), n_batch)
            n_re = a_re * s_re - a_im * s_im + buf[rows, :half]
            n_im = a_re * s_im + a_im * s_re + buf[rows, half:]
            buf[rows, :half] = n_re
            buf[rows, half:] = n_im
            return n_re, n_im

        s_re, s_im = lax.fori_loop(0, chunk, step, (st_re[j], st_im[j]), unroll=True)
        st_re[j] = s_re
        st_im[j] = s_im
        y_scr[:, ch] = jnp.dot(buf[...].astype(BF16), cmat_ref[j], preferred_element_type=F32)

    y = _gelu_tanh(y_scr[...] + dskip_ref[...] * u_scr[...].astype(F32))
    gate = jnp.dot(y.astype(BF16), wglu_ref[...], preferred_element_type=F32) + bglu_ref[...]
    y = (y * jax.nn.sigmoid(gate)).astype(BF16)
    for g in range(n_perm):
        yg = jnp.dot(perm_t_ref[...], y[g * perm_rows:(g + 1) * perm_rows, :], preferred_element_type=F32)
        o_ref[:, g * SSM_PERM_T:(g + 1) * SSM_PERM_T, :] = yg.reshape(
            n_batch, SSM_PERM_T, yg.shape[1]).astype(BF16)


def _ssm_params(lam_re, lam_im, log_dt, b_re, b_im, c_re, c_im):
    groups = lam_re.shape[0]
    per_blk = SSM_CH_BLOCK // SSM_GROUP_CH
    n_blk = groups // per_blk
    dt = jnp.exp(log_dt.astype(F32))[:, None]
    lam_re = lam_re.astype(F32)
    lam_im = lam_im.astype(F32)
    mag = jnp.exp(lam_re * dt)
    bar_re = mag * jnp.cos(lam_im * dt)
    bar_im = mag * jnp.sin(lam_im * dt)
    den = lam_re * lam_re + lam_im * lam_im
    gain_re = ((bar_re - 1.0) * lam_re + bar_im * lam_im) / den
    gain_im = (bar_im * lam_re - (bar_re - 1.0) * lam_im) / den
    bb_re = gain_re[..., None] * b_re - gain_im[..., None] * b_im
    bb_im = gain_re[..., None] * b_im + gain_im[..., None] * b_re
    eye = jnp.eye(per_blk, dtype=F32)

    def in_blocks(m):
        m = m.reshape(n_blk, per_blk, SSM_STATE, SSM_GROUP_CH)
        return jnp.einsum('jgnc,gh->jgchn', m, eye).reshape(n_blk, SSM_CH_BLOCK, per_blk * SSM_STATE)

    def out_blocks(m):
        m = m.reshape(n_blk, per_blk, SSM_GROUP_CH, SSM_STATE)
        return jnp.einsum('jgcn,gh->jhngc', m, eye).reshape(n_blk, per_blk * SSM_STATE, SSM_CH_BLOCK)

    bmat = jnp.concatenate([in_blocks(bb_re), in_blocks(bb_im)], axis=2).astype(BF16)
    cmat = jnp.concatenate([out_blocks(c_re.astype(F32)), out_blocks(-c_im.astype(F32))], axis=1).astype(BF16)
    a_re = bar_re.reshape(n_blk, 1, per_blk * SSM_STATE)
    a_im = bar_im.reshape(n_blk, 1, per_blk * SSM_STATE)
    return bmat, cmat, a_re, a_im


def _time_major_perm(n_batch):
    rows = n_batch * SSM_PERM_T
    perm = np.zeros((rows, rows), np.float32)
    for bi in range(n_batch):
        for tl in range(SSM_PERM_T):
            perm[tl * n_batch + bi, bi * SSM_PERM_T + tl] = 1.0
    return perm


def _ssm_call(x, w_u, bmat, cmat, a_re, a_im, d_skip, w_glu, b_glu):
    b, s, d = x.shape
    width = w_u.shape[1]
    chunk = min(SSM_CHUNK, s)
    rows = b * chunk
    n_blk, _, n_state2 = bmat.shape
    perm = _time_major_perm(b)
    const = lambda *shape: pl.BlockSpec(shape, lambda i: (0,) * len(shape))
    return pl.pallas_call(
        _ssm_kernel,
        grid=(s // chunk,),
        in_specs=[pl.BlockSpec((b, chunk, d), lambda i: (0, i, 0)),
                  const(d, width), const(*perm.shape), const(*perm.shape),
                  const(*bmat.shape), const(*cmat.shape), const(*a_re.shape), const(*a_im.shape),
                  const(1, width), const(width, width), const(1, width)],
        out_specs=pl.BlockSpec((b, chunk, width), lambda i: (0, i, 0)),
        out_shape=jax.ShapeDtypeStruct((b, s, width), BF16),
        scratch_shapes=[pltpu.VMEM((n_blk, b, n_state2 // 2), F32),
                        pltpu.VMEM((n_blk, b, n_state2 // 2), F32),
                        pltpu.VMEM((rows, width), BF16),
                        pltpu.VMEM((2, rows, n_state2), F32),
                        pltpu.VMEM((rows, width), F32)],
        compiler_params=_params(("arbitrary",)),
        name="s5_mixer",
    )(x, w_u, jnp.asarray(perm, BF16), jnp.asarray(perm.T, BF16), bmat, cmat, a_re, a_im,
      d_skip, w_glu, b_glu)


def _layer_norm(v, g, b):
    mu = jnp.mean(v, axis=-1, keepdims=True)
    c = v - mu
    var = jnp.mean(c * c, axis=-1, keepdims=True)
    return c * lax.rsqrt(var + LN_EPS) * g + b


def _router_gates(scores, sel):
    rows = scores.shape[1]
    per_group = N_EXPERTS // N_EXPERT_GROUPS
    neg_inf = -jnp.inf
    sel3 = sel.reshape(N_EXPERT_GROUPS, per_group, rows)
    sub = lax.broadcasted_iota(jnp.int32, sel3.shape, 1)
    m1 = jnp.max(sel3, axis=1, keepdims=True)
    first = jnp.min(jnp.where(sel3 == m1, sub, per_group), axis=1, keepdims=True)
    m2 = jnp.max(jnp.where(sub == first, neg_inf, sel3), axis=1, keepdims=True)
    grp = jnp.broadcast_to(m1 + m2, sel3.shape)
    kept = []
    for g in range(N_EXPERT_GROUPS):
        beaten = jnp.zeros(grp.shape[1:], F32)
        for o in range(N_EXPERT_GROUPS):
            if o == g:
                continue
            wins = (grp[o] >= grp[g]) if o < g else (grp[o] > grp[g])
            beaten = beaten + jnp.where(wins, 1.0, 0.0)
        kept.append(jnp.where(beaten < TOPK_GROUPS, sel3[g], neg_inf))
    work = jnp.stack(kept, axis=0).reshape(N_EXPERTS, rows)
    eidx = lax.broadcasted_iota(jnp.int32, work.shape, 0)
    w = jnp.zeros(work.shape, F32)
    for _ in range(TOP_K):
        m = jnp.max(work, axis=0, keepdims=True)
        pick = jnp.min(jnp.where(work == m, eidx, N_EXPERTS), axis=0, keepdims=True)
        hit = eidx == pick
        w = jnp.where(hit, scores, w)
        work = jnp.where(hit, neg_inf, work)
    return w / jnp.sum(w, axis=0, keepdims=True) * ROUTED_SCALE


def _post_kernel(alpha, x_ref, attn_ref, ssm_ref, p_ref, wout_ref, g1_ref, b1_ref, wr_hi_ref, wr_lo_ref,
                 rbias_ref, wsg_ref, wsu_ref, wsd_ref, wple_ref, wpg_ref, r_ref, h16_ref, gates_ref):
    n_pairs, _, lanes = attn_ref.shape[1:]
    mix = jnp.dot(ssm_ref[...], wout_ref[n_pairs * lanes:, :], preferred_element_type=F32)
    for pair in range(0, n_pairs, 2):
        a2 = jnp.concatenate([attn_ref[0, pair], attn_ref[0, pair + 1]], axis=1)
        mix = mix + jnp.dot(a2, wout_ref[pair * lanes:(pair + 2) * lanes, :], preferred_element_type=F32)
    h = _layer_norm(alpha * x_ref[...] + mix, g1_ref[...], b1_ref[...])
    h16 = h.astype(BF16)
    h16_ref[...] = h16
    h_lo = (h - h16.astype(F32)).astype(BF16)
    logits = _dot_nt(wr_hi_ref[...], h16) + _dot_nt(wr_hi_ref[...], h_lo) + _dot_nt(wr_lo_ref[...], h16)
    sg = jnp.dot(h16, wsg_ref[...], preferred_element_type=F32)
    su = jnp.dot(h16, wsu_ref[...], preferred_element_type=F32)
    pg = jnp.dot(h16, wpg_ref[...], preferred_element_type=F32)
    pe = jnp.dot(p_ref[...].astype(BF16), wple_ref[...], preferred_element_type=F32)
    scores = jax.nn.sigmoid(logits)
    gates = _router_gates(scores, scores + rbias_ref[...])
    gates = jnp.concatenate([gates, jnp.zeros_like(gates)], axis=0)
    gates_ref[...] = gates.T
    shared = jnp.dot((sg * jax.nn.sigmoid(sg) * su).astype(BF16), wsd_ref[...], preferred_element_type=F32)
    r_ref[...] = alpha * h + shared + pe * jax.nn.sigmoid(pg)


def _post_call(alpha, x2, attn, ssm2, p2, w_out, g1, b1, wr_hi, wr_lo, rbias, wsg, wsu, wsd, wple, wpg):
    t, d = x2.shape
    _, n_pairs, s, lanes = attn.shape
    rows = min(POST_ROWS, s)
    per_seq = s // rows
    row_blk = lambda w: pl.BlockSpec((rows, w), lambda i: (i, 0))
    const = lambda a: pl.BlockSpec(a.shape, lambda i: (0,) * a.ndim)
    attn_blk = pl.BlockSpec((1, n_pairs, rows, lanes), lambda i: (i // per_seq, 0, i % per_seq, 0))
    weights = (w_out, g1, b1, wr_hi, wr_lo, rbias, wsg, wsu, wsd, wple, wpg)
    return pl.pallas_call(
        functools.partial(_post_kernel, alpha),
        grid=(t // rows,),
        in_specs=[row_blk(d), attn_blk, row_blk(ssm2.shape[1]), row_blk(p2.shape[1])]
                 + [const(w) for w in weights],
        out_specs=[row_blk(d), row_blk(d), row_blk(2 * N_EXPERTS)],
        out_shape=[jax.ShapeDtypeStruct((t, d), F32), jax.ShapeDtypeStruct((t, d), BF16),
                   jax.ShapeDtypeStruct((t, 2 * N_EXPERTS), F32)],
        compiler_params=_params(("arbitrary",)),
        name="post_mix_router",
    )(x2, attn, ssm2, p2, *weights)


def _moe_kernel(h_ref, gates_ref, r_ref, wg_ref, wu_ref, wd_ref, g2_ref, b2_ref, o_ref, acc_ref, hid_ref):
    step = pl.program_id(1)
    n_exp, _, hidden = wg_ref.shape

    @pl.when(step == 0)
    def _():
        acc_ref[...] = jnp.zeros_like(acc_ref)

    lanes = gates_ref.shape[1]
    g = pltpu.roll(gates_ref[...], (lanes - n_exp * step) % lanes, axis=1)
    h = h_ref[...]
    def project(j):
        return (jnp.dot(h, wg_ref[j], preferred_element_type=F32),
                jnp.dot(h, wu_ref[j], preferred_element_type=F32))

    z_next = project(0)
    for j in range(n_exp):
        zg, zu = z_next
        if j + 1 < n_exp:
            z_next = project(j + 1)
        hid = zg * jax.nn.sigmoid(zg) * zu * g[:, j:j + 1]
        hid_ref[:, j * hidden:(j + 1) * hidden] = hid.astype(BF16)
    wd = wd_ref[...].reshape(n_exp * hidden, wd_ref.shape[2])
    acc_ref[...] += jnp.dot(hid_ref[...], wd, preferred_element_type=F32)

    @pl.when(step == pl.num_programs(1) - 1)
    def _():
        o_ref[...] = _layer_norm(r_ref[...] + acc_ref[...], g2_ref[...], b2_ref[...])


def _moe_call(h16, gates, r, wg, wu, wd, g2, b2):
    t, d = h16.shape
    rows = min(MOE_ROWS, t)
    n_exp = MOE_EXPERTS_PER_STEP
    hidden = wg.shape[2]
    row_blk = lambda w: pl.BlockSpec((rows, w), lambda i, e: (i, 0))
    return pl.pallas_call(
        _moe_kernel,
        grid=(t // rows, wg.shape[0] // n_exp),
        in_specs=[row_blk(d), row_blk(gates.shape[1]), row_blk(d),
                  pl.BlockSpec((n_exp, d, hidden), lambda i, e: (e, 0, 0)),
                  pl.BlockSpec((n_exp, d, hidden), lambda i, e: (e, 0, 0)),
                  pl.BlockSpec((n_exp, hidden, d), lambda i, e: (e, 0, 0)),
                  pl.BlockSpec((1, d), lambda i, e: (0, 0)),
                  pl.BlockSpec((1, d), lambda i, e: (0, 0))],
        out_specs=row_blk(d),
        out_shape=jax.ShapeDtypeStruct((t, d), F32),
        scratch_shapes=[pltpu.VMEM((rows, d), F32), pltpu.VMEM((rows, n_exp * hidden), BF16)],
        compiler_params=_params(("arbitrary", "arbitrary")),
        name="routed_experts",
    )(h16, gates, r, wg, wu, wd, g2, b2)


def _layer(h, p_i, w_in, lam_re, lam_im, log_dt, b_re, b_im, c_re, c_im, d_skip, w_glu, b_glu, w_out,
           ln1_g, ln1_b, w_router, router_bias, w_gate, w_up, w_down, ws_gate, ws_up, ws_down, w_ple,
           w_ple_gate, ln2_g, ln2_b, alpha):
    b, s, d = h.shape
    t = b * s
    row = lambda v: v.reshape(1, -1).astype(F32)

    q, k, v = _qkv_call(h, _qkv_weights(w_in), jnp.asarray(_qkv_tables(s), BF16))
    attn = _attn_call(q, k, v, jnp.asarray(_log_multiplicity_table()))

    bmat, cmat, a_re, a_im = _ssm_params(lam_re, lam_im, log_dt, b_re, b_im, c_re, c_im)
    ssm = _ssm_call(h, w_in[:, 3 * ATTN_WIDTH:].astype(BF16), bmat, cmat, a_re, a_im,
                    row(d_skip), w_glu.astype(BF16), row(b_glu))

    wr_t = w_router.astype(F32).T
    wr_hi = wr_t.astype(BF16)
    wr_lo = (wr_t - wr_hi.astype(F32)).astype(BF16)
    r, h16, gates = _post_call(
        alpha, h.reshape(t, d), attn, ssm.reshape(t, -1), p_i.reshape(t, -1),
        w_out.astype(BF16), row(ln1_g), row(ln1_b), wr_hi, wr_lo, router_bias.reshape(-1, 1).astype(F32),
        ws_gate.astype(BF16), ws_up.astype(BF16), ws_down.astype(BF16), w_ple.astype(BF16),
        w_ple_gate.astype(BF16))
    out = _moe_call(h16, gates, r, w_gate.astype(BF16), w_up.astype(BF16), w_down.astype(BF16),
                    row(ln2_g), row(ln2_b))
    return out.reshape(b, s, d)


def kernel(x, p, w_in, lam_re, lam_im, log_dt, b_re, b_im, c_re, c_im, d_skip, w_glu, b_glu, w_out, ln1_g, ln1_b, w_router, router_bias, w_gate, w_up, w_down, ws_gate, ws_up, ws_down, w_ple, w_ple_gate, ln2_g, ln2_b):
    depth = w_in.shape[0]
    alpha = (2.0 * depth) ** 0.25
    h = x
    for i in range(depth):
        h = _layer(h, p[i], w_in[i], lam_re[i], lam_im[i], log_dt[i], b_re[i], b_im[i], c_re[i], c_im[i],
                   d_skip[i], w_glu[i], b_glu[i], w_out[i], ln1_g[i], ln1_b[i], w_router[i], router_bias[i],
                   w_gate[i], w_up[i], w_down[i], ws_gate[i], ws_up[i], ws_down[i], w_ple[i], w_ple_gate[i],
                   ln2_g[i], ln2_b[i], alpha)
    return h
```

```python
import functools
import math

import numpy as np
import jax
import jax.numpy as jnp
from jax import lax
from jax.experimental import pallas as pl
from jax.experimental.pallas import tpu as pltpu

F32 = jnp.float32
BF16 = jnp.bfloat16

HEAD_DIM = 64
N_HEADS = 8
ATTN_WIDTH = N_HEADS * HEAD_DIM
HEAD_LANES = 128
SSM_GROUP_CH = 16
SSM_STATE = 64
DILATED_BRANCHES = ((128, 1), (512, 4), (2048, 16))
N_EXPERTS = 64
TOP_K = 8
N_EXPERT_GROUPS = 8
TOPK_GROUPS = 4
ROUTED_SCALE = 2.5
LN_EPS = 1e-5
MASK_VALUE = -1e30

ATT_BLOCK = 256
ATT_LOOKBACK = max(w for w, _ in DILATED_BRANCHES) // ATT_BLOCK
ATT_WINDOWS = (2, 4, 6, ATT_LOOKBACK + 1)
QKV_ROWS = 1024
SSM_CHUNK = 128
SSM_PERM_T = 32
SSM_CH_BLOCK = 128
POST_ROWS = 1024
MOE_ROWS = 1024
MOE_EXPERTS_PER_STEP = 4
VMEM_LIMIT = 56 * 1024 * 1024


def _params(sem, vmem=VMEM_LIMIT):
    return pltpu.CompilerParams(dimension_semantics=sem, vmem_limit_bytes=vmem)


def _qkv_kernel(x_ref, w_ref, pos_ref, q_ref, k_ref, v_ref):
    z = jnp.dot(x_ref[0].astype(BF16), w_ref[...], preferred_element_type=F32).astype(BF16)
    lane = lax.broadcasted_iota(jnp.int32, (z.shape[0], HEAD_LANES), 1)
    for c, o_ref in enumerate((q_ref, k_ref, v_ref)):
        for h in range(N_HEADS):
            zc = z[:, c * ATTN_WIDTH + (h // 2) * HEAD_LANES:c * ATTN_WIDTH + (h // 2 + 1) * HEAD_LANES]
            pos = pos_ref[:, (c * N_HEADS + h) * HEAD_LANES:(c * N_HEADS + h + 1) * HEAD_LANES]
            data = (lane < HEAD_DIM) if h % 2 == 0 else (lane >= HEAD_DIM)
            o_ref[0, h] = jnp.where(data, zc, pos)


def _qkv_tables(seq):
    t = np.arange(seq)
    width = N_HEADS * HEAD_LANES
    qpos = np.zeros((seq, width), np.float32)
    kpos = np.zeros((seq, width), np.float32)
    vpos = np.zeros((seq, width), np.float32)
    for h in range(N_HEADS):
        slope = 2.0 ** (-8.0 * (h + 1) / N_HEADS)
        base = h * HEAD_LANES + (HEAD_DIM if h % 2 == 0 else 0)
        qpos[:, base + 0] = -slope * (t % ATT_BLOCK)
        qpos[:, base + 1] = -slope * ATT_BLOCK * (t // ATT_BLOCK)
        qpos[:, base + 2] = slope
        qpos[:, base + 3] = slope * ATT_BLOCK
        kpos[:, base + 0] = 1.0
        kpos[:, base + 1] = 1.0
        kpos[:, base + 2] = t % ATT_BLOCK
        kpos[:, base + 3] = t // ATT_BLOCK
        ones_lane = h * HEAD_LANES + (HEAD_DIM if h % 2 == 0 else 0)
        vpos[:, ones_lane] = 1.0
    tables = np.concatenate([qpos, kpos, vpos], axis=1)
    assert np.array_equal(tables.astype(BF16).astype(np.float32), tables), "helper columns must be bf16-exact"
    return tables


def _qkv_weights(w_in):
    wq = w_in[:, :ATTN_WIDTH] * (HEAD_DIM ** -0.5)
    return jnp.concatenate([wq, w_in[:, ATTN_WIDTH:3 * ATTN_WIDTH]], axis=1).astype(BF16)


def _qkv_call(x, w_qkv, pos):
    b, s, d = x.shape
    width = N_HEADS * HEAD_LANES
    rows = min(QKV_ROWS, s)
    out = jax.ShapeDtypeStruct((b, N_HEADS, s, HEAD_LANES), BF16)
    blk = pl.BlockSpec((1, N_HEADS, rows, HEAD_LANES), lambda si, bi: (bi, 0, si, 0))
    return pl.pallas_call(
        _qkv_kernel,
        grid=(s // rows, b),
        in_specs=[pl.BlockSpec((1, rows, d), lambda si, bi: (bi, si, 0)),
                  pl.BlockSpec(w_qkv.shape, lambda si, bi: (0, 0)),
                  pl.BlockSpec((rows, 3 * width), lambda si, bi: (si, 0))],
        out_specs=[blk, blk, blk],
        out_shape=[out, out, out],
        compiler_params=_params(("arbitrary", "arbitrary")),
        name="qkv_proj",
    )(x, w_qkv, pos)


def _log_multiplicity_table():
    a = (ATT_LOOKBACK - np.arange(2 * ATT_LOOKBACK + 1))[:, None, None]
    i = np.arange(ATT_BLOCK)[None, :, None]
    j = np.arange(ATT_BLOCK)[None, None, :]
    dist = ATT_BLOCK * a + i - j
    mult = np.zeros(dist.shape, np.float64)
    for window, dil in DILATED_BRANCHES:
        mult += (dist >= 0) & (dist <= window) & (dist % dil == 0)
    return np.where(mult > 0, np.log(np.maximum(mult, 1.0)), MASK_VALUE).astype(np.float32)


def _dot_nt(a, b):
    return lax.dot_general(a, b, (((1,), (1,)), ((), ())), preferred_element_type=F32)


def _attn_kernel(windows, q_ref, k_ref, v_ref, bias_ref, o_ref):
    qi = pl.program_id(1)
    lo = 0
    for idx, n_win in enumerate(windows):
        last = idx == len(windows) - 1
        cond = (qi >= lo) if last else jnp.logical_and(qi >= lo, qi < n_win)
        pl.when(cond)(functools.partial(_attn_window, n_win, q_ref, k_ref, v_ref, bias_ref, o_ref))
        lo = n_win


def _attn_window(n_win, q_ref, k_ref, v_ref, bias_ref, o_ref):
    qi = pl.program_id(1)
    first = jnp.maximum(qi - (n_win - 1), 0)
    start = pl.multiple_of(first * ATT_BLOCK, ATT_BLOCK)
    u0 = ATT_LOOKBACK - (qi - first)
    lane = lax.broadcasted_iota(jnp.int32, (ATT_BLOCK, HEAD_LANES), 1)

    window = pl.ds(start, n_win * ATT_BLOCK)

    def scores(h):
        s = _dot_nt(q_ref[0, h], k_ref[0, h, window, :])
        return s + jnp.concatenate([bias_ref[u0 + c] for c in range(n_win)], axis=1)

    def finish(h, s):
        p = jnp.exp(s - jnp.max(s, axis=1, keepdims=True))
        acc = jnp.dot(p.astype(BF16), v_ref[0, h, window, :], preferred_element_type=F32)
        ones_lane = HEAD_DIM if h % 2 == 0 else 0
        return acc * (1.0 / acc[:, ones_lane:ones_lane + 1])

    s_next = scores(0)
    outs = []
    for h in range(N_HEADS):
        s_cur = s_next
        if h + 1 < N_HEADS:
            s_next = scores(h + 1)
        outs.append(finish(h, s_cur))
        if h % 2 == 1:
            o_ref[0, h // 2] = jnp.where(lane < HEAD_DIM, outs[h - 1], outs[h]).astype(BF16)


def _attn_call(q, k, v, bias):
    b, n_heads, s, lanes = q.shape
    n_win = min(ATT_LOOKBACK + 1, s // ATT_BLOCK)
    windows = tuple(sorted({min(w, n_win) for w in ATT_WINDOWS}))
    once = pl.Buffered(1)
    return pl.pallas_call(
        functools.partial(_attn_kernel, windows),
        grid=(b, s // ATT_BLOCK),
        in_specs=[pl.BlockSpec((1, n_heads, ATT_BLOCK, lanes), lambda bi, qi: (bi, 0, qi, 0)),
                  pl.BlockSpec((1, n_heads, s, lanes), lambda bi, qi: (bi, 0, 0, 0), pipeline_mode=once),
                  pl.BlockSpec((1, n_heads, s, lanes), lambda bi, qi: (bi, 0, 0, 0), pipeline_mode=once),
                  pl.BlockSpec(bias.shape, lambda bi, qi: (0, 0, 0), pipeline_mode=once)],
        out_specs=pl.BlockSpec((1, n_heads // 2, ATT_BLOCK, lanes), lambda bi, qi: (bi, 0, qi, 0)),
        out_shape=jax.ShapeDtypeStruct((b, n_heads // 2, s, lanes), BF16),
        compiler_params=_params(("arbitrary", "arbitrary")),
        name="banded_attention",
    )(q, k, v, bias)


def _gelu_tanh(y):
    return 0.5 * y * (1.0 + jnp.tanh(math.sqrt(2.0 / math.pi) * (y + 0.044715 * (y * y * y))))


def _ssm_kernel(x_ref, wu_ref, perm_ref, perm_t_ref, bmat_ref, cmat_ref, a_re_ref, a_im_ref, dskip_ref,
                wglu_ref, bglu_ref, o_ref, st_re, st_im, u_scr, bu_scr, y_scr):
    n_batch, chunk, _ = x_ref.shape
    n_perm = chunk // SSM_PERM_T
    perm_rows = n_batch * SSM_PERM_T
    n_blocks = bmat_ref.shape[0]
    half = bmat_ref.shape[2] // 2

    @pl.when(pl.program_id(0) == 0)
    def _():
        st_re[...] = jnp.zeros_like(st_re)
        st_im[...] = jnp.zeros_like(st_im)

    for g in range(n_perm):
        xg = x_ref[:, g * SSM_PERM_T:(g + 1) * SSM_PERM_T, :].reshape(perm_rows, x_ref.shape[2])
        ug = jnp.dot(xg.astype(BF16), wu_ref[...], preferred_element_type=F32).astype(BF16)
        u_scr[g * perm_rows:(g + 1) * perm_rows, :] = jnp.dot(
            perm_ref[...], ug, preferred_element_type=F32).astype(BF16)

    def drive(j):
        ch = slice(j * SSM_CH_BLOCK, (j + 1) * SSM_CH_BLOCK)
        bu_scr[j % 2] = jnp.dot(u_scr[:, ch], bmat_ref[j], preferred_element_type=F32)

    drive(0)
    for j in range(n_blocks):
        ch = slice(j * SSM_CH_BLOCK, (j + 1) * SSM_CH_BLOCK)
        if j + 1 < n_blocks:
            drive(j + 1)
        buf = bu_scr.at[j % 2]
        a_re = jnp.broadcast_to(a_re_ref[j], (n_batch, half))
        a_im = jnp.broadcast_to(a_im_ref[j], (n_batch, half))

        def step(t, carry, a_re=a_re, a_im=a_im, buf=buf):
            s_re, s_im = carry
            rows = pl.ds(pl.multiple_of(t * n_batch, n_batch), n_batch)
            n_re = a_re * s_re - a_im * s_im + buf[rows, :half]
            n_im = a_re * s_im + a_im * s_re + buf[rows, half:]
            buf[rows, :half] = n_re
            buf[rows, half:] = n_im
            return n_re, n_im

        s_re, s_im = lax.fori_loop(0, chunk, step, (st_re[j], st_im[j]), unroll=True)
        st_re[j] = s_re
        st_im[j] = s_im
        y_scr[:, ch] = jnp.dot(buf[...].astype(BF16), cmat_ref[j], preferred_element_type=F32)

    y = _gelu_tanh(y_scr[...] + dskip_ref[...] * u_scr[...].astype(F32))
    gate = jnp.dot(y.astype(BF16), wglu_ref[...], preferred_element_type=F32) + bglu_ref[...]
    y = (y * jax.nn.sigmoid(gate)).astype(BF16)
    for g in range(n_perm):
        yg = jnp.dot(perm_t_ref[...], y[g * perm_rows:(g + 1) * perm_rows, :], preferred_element_type=F32)
        o_ref[:, g * SSM_PERM_T:(g + 1) * SSM_PERM_T, :] = yg.reshape(
            n_batch, SSM_PERM_T, yg.shape[1]).astype(BF16)


def _ssm_params(lam_re, lam_im, log_dt, b_re, b_im, c_re, c_im):
    groups = lam_re.shape[0]
    per_blk = SSM_CH_BLOCK // SSM_GROUP_CH
    n_blk = groups // per_blk
    dt = jnp.exp(log_dt.astype(F32))[:, None]
    lam_re = lam_re.astype(F32)
    lam_im = lam_im.astype(F32)
    mag = jnp.exp(lam_re * dt)
    bar_re = mag * jnp.cos(lam_im * dt)
    bar_im = mag * jnp.sin(lam_im * dt)
    den = lam_re * lam_re + lam_im * lam_im
    gain_re = ((bar_re - 1.0) * lam_re + bar_im * lam_im) / den
    gain_im = (bar_im * lam_re - (bar_re - 1.0) * lam_im) / den
    bb_re = gain_re[..., None] * b_re - gain_im[..., None] * b_im
    bb_im = gain_re[..., None] * b_im + gain_im[..., None] * b_re
    eye = jnp.eye(per_blk, dtype=F32)

    def in_blocks(m):
        m = m.reshape(n_blk, per_blk, SSM_STATE, SSM_GROUP_CH)
        return jnp.einsum('jgnc,gh->jgchn', m, eye).reshape(n_blk, SSM_CH_BLOCK, per_blk * SSM_STATE)

    def out_blocks(m):
        m = m.reshape(n_blk, per_blk, SSM_GROUP_CH, SSM_STATE)
        return jnp.einsum('jgcn,gh->jhngc', m, eye).reshape(n_blk, per_blk * SSM_STATE, SSM_CH_BLOCK)

    bmat = jnp.concatenate([in_blocks(bb_re), in_blocks(bb_im)], axis=2).astype(BF16)
    cmat = jnp.concatenate([out_blocks(c_re.astype(F32)), out_blocks(-c_im.astype(F32))], axis=1).astype(BF16)
    a_re = bar_re.reshape(n_blk, 1, per_blk * SSM_STATE)
    a_im = bar_im.reshape(n_blk, 1, per_blk * SSM_STATE)
    return bmat, cmat, a_re, a_im


def _time_major_perm(n_batch):
    rows = n_batch * SSM_PERM_T
    perm = np.zeros((rows, rows), np.float32)
    for bi in range(n_batch):
        for tl in range(SSM_PERM_T):
            perm[tl * n_batch + bi, bi * SSM_PERM_T + tl] = 1.0
    return perm


def _ssm_call(x, w_u, bmat, cmat, a_re, a_im, d_skip, w_glu, b_glu):
    b, s, d = x.shape
    width = w_u.shape[1]
    chunk = min(SSM_CHUNK, s)
    rows = b * chunk
    n_blk, _, n_state2 = bmat.shape
    perm = _time_major_perm(b)
    const = lambda *shape: pl.BlockSpec(shape, lambda i: (0,) * len(shape))
    return pl.pallas_call(
        _ssm_kernel,
        grid=(s // chunk,),
        in_specs=[pl.BlockSpec((b, chunk, d), lambda i: (0, i, 0)),
                  const(d, width), const(*perm.shape), const(*perm.shape),
                  const(*bmat.shape), const(*cmat.shape), const(*a_re.shape), const(*a_im.shape),
                  const(1, width), const(width, width), const(1, width)],
        out_specs=pl.BlockSpec((b, chunk, width), lambda i: (0, i, 0)),
        out_shape=jax.ShapeDtypeStruct((b, s, width), BF16),
        scratch_shapes=[pltpu.VMEM((n_blk, b, n_state2 // 2), F32),
                        pltpu.VMEM((n_blk, b, n_state2 // 2), F32),
                        pltpu.VMEM((rows, width), BF16),
                        pltpu.VMEM((2, rows, n_state2), F32),
                        pltpu.VMEM((rows, width), F32)],
        compiler_params=_params(("arbitrary",)),
        name="s5_mixer",
    )(x, w_u, jnp.asarray(perm, BF16), jnp.asarray(perm.T, BF16), bmat, cmat, a_re, a_im,
      d_skip, w_glu, b_glu)


def _layer_norm(v, g, b):
    mu = jnp.mean(v, axis=-1, keepdims=True)
    c = v - mu
    var = jnp.mean(c * c, axis=-1, keepdims=True)
    return c * lax.rsqrt(var + LN_EPS) * g + b


def _router_gates(scores, sel):
    rows = scores.shape[1]
    per_group = N_EXPERTS // N_EXPERT_GROUPS
    neg_inf = -jnp.inf
    sel3 = sel.reshape(N_EXPERT_GROUPS, per_group, rows)
    sub = lax.broadcasted_iota(jnp.int32, sel3.shape, 1)
    m1 = jnp.max(sel3, axis=1, keepdims=True)
    first = jnp.min(jnp.where(sel3 == m1, sub, per_group), axis=1, keepdims=True)
    m2 = jnp.max(jnp.where(sub == first, neg_inf, sel3), axis=1, keepdims=True)
    grp = jnp.broadcast_to(m1 + m2, sel3.shape)
    kept = []
    for g in range(N_EXPERT_GROUPS):
        beaten = jnp.zeros(grp.shape[1:], F32)
        for o in range(N_EXPERT_GROUPS):
            if o == g:
                continue
            wins = (grp[o] >= grp[g]) if o < g else (grp[o] > grp[g])
            beaten = beaten + jnp.where(wins, 1.0, 0.0)
        kept.append(jnp.where(beaten < TOPK_GROUPS, sel3[g], neg_inf))
    work = jnp.stack(kept, axis=0).reshape(N_EXPERTS, rows)
    eidx = lax.broadcasted_iota(jnp.int32, work.shape, 0)
    w = jnp.zeros(work.shape, F32)
    for _ in range(TOP_K):
        m = jnp.max(work, axis=0, keepdims=True)
        pick = jnp.min(jnp.where(work == m, eidx, N_EXPERTS), axis=0, keepdims=True)
        hit = eidx == pick
        w = jnp.where(hit, scores, w)
        work = jnp.where(hit, neg_inf, work)
    return w / jnp.sum(w, axis=0, keepdims=True) * ROUTED_SCALE


def _post_kernel(alpha, x_ref, attn_ref, ssm_ref, p_ref, wout_ref, g1_ref, b1_ref, wr_hi_ref, wr_lo_ref,
                 rbias_ref, wsg_ref, wsu_ref, wsd_ref, wple_ref, wpg_ref, r_ref, h16_ref, gates_ref):
    n_pairs, _, lanes = attn_ref.shape[1:]
    n_rows = x_ref.shape[0]
    halves = [slice(k * n_rows // 2, (k + 1) * n_rows // 2) for k in range(2)]
    st = [dict(), dict()]

    def mix_stage(k):
        rs, s = halves[k], st[k]
        mix = jnp.dot(ssm_ref[rs, :], wout_ref[n_pairs * lanes:, :], preferred_element_type=F32)
        for pair in range(0, n_pairs, 2):
            a2 = jnp.concatenate([attn_ref[0, pair, rs, :], attn_ref[0, pair + 1, rs, :]], axis=1)
            mix = mix + jnp.dot(a2, wout_ref[pair * lanes:(pair + 2) * lanes, :], preferred_element_type=F32)
        s["mix"] = mix
        s["pe"] = jnp.dot(p_ref[rs, :].astype(BF16), wple_ref[...], preferred_element_type=F32)

    def norm_stage(k):
        rs, s = halves[k], st[k]
        h = _layer_norm(alpha * x_ref[rs, :] + s["mix"], g1_ref[...], b1_ref[...])
        h16 = h.astype(BF16)
        h16_ref[rs, :] = h16
        s["h"], s["h16"] = h, h16
        s["h_lo"] = (h - h16.astype(F32)).astype(BF16)

    def proj_stage(k):
        s = st[k]
        h16 = s["h16"]
        s["logits"] = (_dot_nt(wr_hi_ref[...], h16) + _dot_nt(wr_hi_ref[...], s["h_lo"])
                       + _dot_nt(wr_lo_ref[...], h16))
        s["sg"] = jnp.dot(h16, wsg_ref[...], preferred_element_type=F32)
        s["su"] = jnp.dot(h16, wsu_ref[...], preferred_element_type=F32)
        s["pg"] = jnp.dot(h16, wpg_ref[...], preferred_element_type=F32)

    def route_stage(k):
        rs, s = halves[k], st[k]
        scores = jax.nn.sigmoid(s["logits"])
        gates = _router_gates(scores, scores + rbias_ref[...])
        gates = jnp.concatenate([gates, jnp.zeros_like(gates)], axis=0)
        gates_ref[rs, :] = gates.T

    def out_stage(k):
        rs, s = halves[k], st[k]
        sg = s["sg"]
        shared = jnp.dot((sg * jax.nn.sigmoid(sg) * s["su"]).astype(BF16), wsd_ref[...],
                         preferred_element_type=F32)
        r_ref[rs, :] = alpha * s["h"] + shared + s["pe"] * jax.nn.sigmoid(s["pg"])

    mix_stage(0)
    mix_stage(1)
    norm_stage(0)
    proj_stage(0)
    norm_stage(1)
    route_stage(0)
    proj_stage(1)
    out_stage(0)
    route_stage(1)
    out_stage(1)


def _post_call(alpha, x2, attn, ssm2, p2, w_out, g1, b1, wr_hi, wr_lo, rbias, wsg, wsu, wsd, wple, wpg):
    t, d = x2.shape
    _, n_pairs, s, lanes = attn.shape
    rows = min(POST_ROWS, s)
    per_seq = s // rows
    row_blk = lambda w: pl.BlockSpec((rows, w), lambda i: (i, 0))
    const = lambda a: pl.BlockSpec(a.shape, lambda i: (0,) * a.ndim)
    attn_blk = pl.BlockSpec((1, n_pairs, rows, lanes), lambda i: (i // per_seq, 0, i % per_seq, 0))
    weights = (w_out, g1, b1, wr_hi, wr_lo, rbias, wsg, wsu, wsd, wple, wpg)
    return pl.pallas_call(
        functools.partial(_post_kernel, alpha),
        grid=(t // rows,),
        in_specs=[row_blk(d), attn_blk, row_blk(ssm2.shape[1]), row_blk(p2.shape[1])]
                 + [const(w) for w in weights],
        out_specs=[row_blk(d), row_blk(d), row_blk(2 * N_EXPERTS)],
        out_shape=[jax.ShapeDtypeStruct((t, d), F32), jax.ShapeDtypeStruct((t, d), BF16),
                   jax.ShapeDtypeStruct((t, 2 * N_EXPERTS), F32)],
        compiler_params=_params(("arbitrary",)),
        name="post_mix_router",
    )(x2, attn, ssm2, p2, *weights)


def _moe_kernel(h_ref, gates_ref, r_ref, wg_ref, wu_ref, wd_ref, g2_ref, b2_ref, o_ref, acc_ref, hid_ref):
    step = pl.program_id(1)
    n_exp, _, hidden = wg_ref.shape

    @pl.when(step == 0)
    def _():
        acc_ref[...] = jnp.zeros_like(acc_ref)

    lanes = gates_ref.shape[1]
    g = pltpu.roll(gates_ref[...], (lanes - n_exp * step) % lanes, axis=1)
    h = h_ref[...]

    def project(j):
        return (jnp.dot(h, wg_ref[j], preferred_element_type=F32),
                jnp.dot(h, wu_ref[j], preferred_element_type=F32))

    z_next = project(0)
    for j in range(n_exp):
        zg, zu = z_next
        if j + 1 < n_exp:
            z_next = project(j + 1)
        hid = zg * jax.nn.sigmoid(zg) * zu * g[:, j:j + 1]
        hid_ref[:, j * hidden:(j + 1) * hidden] = hid.astype(BF16)
    wd = wd_ref[...].reshape(n_exp * hidden, wd_ref.shape[2])
    acc_ref[...] += jnp.dot(hid_ref[...], wd, preferred_element_type=F32)

    @pl.when(step == pl.num_programs(1) - 1)
    def _():
        o_ref[...] = _layer_norm(r_ref[...] + acc_ref[...], g2_ref[...], b2_ref[...])


def _moe_call(h16, gates, r, wg, wu, wd, g2, b2):
    t, d = h16.shape
    rows = min(MOE_ROWS, t)
    n_exp = MOE_EXPERTS_PER_STEP
    hidden = wg.shape[2]
    row_blk = lambda w: pl.BlockSpec((rows, w), lambda i, e: (i, 0))
    return pl.pallas_call(
        _moe_kernel,
        grid=(t // rows, wg.shape[0] // n_exp),
        in_specs=[row_blk(d), row_blk(gates.shape[1]), row_blk(d),
                  pl.BlockSpec((n_exp, d, hidden), lambda i, e: (e, 0, 0)),
                  pl.BlockSpec((n_exp, d, hidden), lambda i, e: (e, 0, 0)),
                  pl.BlockSpec((n_exp, hidden, d), lambda i, e: (e, 0, 0)),
                  pl.BlockSpec((1, d), lambda i, e: (0, 0)),
                  pl.BlockSpec((1, d), lambda i, e: (0, 0))],
        out_specs=row_blk(d),
        out_shape=jax.ShapeDtypeStruct((t, d), F32),
        scratch_shapes=[pltpu.VMEM((rows, d), F32), pltpu.VMEM((rows, n_exp * hidden), BF16)],
        compiler_params=_params(("arbitrary", "arbitrary")),
        name="routed_experts",
    )(h16, gates, r, wg, wu, wd, g2, b2)


def _layer(h, p_i, w_in, lam_re, lam_im, log_dt, b_re, b_im, c_re, c_im, d_skip, w_glu, b_glu, w_out,
           ln1_g, ln1_b, w_router, router_bias, w_gate, w_up, w_down, ws_gate, ws_up, ws_down, w_ple,
           w_ple_gate, ln2_g, ln2_b, alpha):
    b, s, d = h.shape
    t = b * s
    row = lambda v: v.reshape(1, -1).astype(F32)

    q, k, v = _qkv_call(h, _qkv_weights(w_in), jnp.asarray(_qkv_tables(s), BF16))
    attn = _attn_call(q, k, v, jnp.asarray(_log_multiplicity_table()))

    bmat, cmat, a_re, a_im = _ssm_params(lam_re, lam_im, log_dt, b_re, b_im, c_re, c_im)
    ssm = _ssm_call(h, w_in[:, 3 * ATTN_WIDTH:].astype(BF16), bmat, cmat, a_re, a_im,
                    row(d_skip), w_glu.astype(BF16), row(b_glu))

    wr_t = w_router.astype(F32).T
    wr_hi = wr_t.astype(BF16)
    wr_lo = (wr_t - wr_hi.astype(F32)).astype(BF16)
    r, h16, gates = _post_call(
        alpha, h.reshape(t, d), attn, ssm.reshape(t, -1), p_i.reshape(t, -1),
        w_out.astype(BF16), row(ln1_g), row(ln1_b), wr_hi, wr_lo, router_bias.reshape(-1, 1).astype(F32),
        ws_gate.astype(BF16), ws_up.astype(BF16), ws_down.astype(BF16), w_ple.astype(BF16),
        w_ple_gate.astype(BF16))
    out = _moe_call(h16, gates, r, w_gate.astype(BF16), w_up.astype(BF16), w_down.astype(BF16),
                    row(ln2_g), row(ln2_b))
    return out.reshape(b, s, d)


def kernel(x, p, w_in, lam_re, lam_im, log_dt, b_re, b_im, c_re, c_im, d_skip, w_glu, b_glu, w_out, ln1_g, ln1_b, w_router, router_bias, w_gate, w_up, w_down, ws_gate, ws_up, ws_down, w_ple, w_ple_gate, ln2_g, ln2_b):
    depth = w_in.shape[0]
    alpha = (2.0 * depth) ** 0.25
    h = x
    for i in range(depth):
        h = _layer(h, p[i], w_in[i], lam_re[i], lam_im[i], log_dt[i], b_re[i], b_im[i], c_re[i], c_im[i],
                   d_skip[i], w_glu[i], b_glu[i], w_out[i], ln1_g[i], ln1_b[i], w_router[i], router_bias[i],
                   w_gate[i], w_up[i], w_down[i], ws_gate[i], ws_up[i], ws_down[i], w_ple[i], w_ple_gate[i],
                   ln2_g[i], ln2_b[i], alpha)
    return h
```

```python
import functools
import math

import numpy as np
import jax
import jax.numpy as jnp
from jax import lax
from jax.experimental import pallas as pl
from jax.experimental.pallas import tpu as pltpu

F32 = jnp.float32
BF16 = jnp.bfloat16

HEAD_DIM = 64
N_HEADS = 8
ATTN_WIDTH = N_HEADS * HEAD_DIM
HEAD_LANES = 128
SSM_GROUP_CH = 16
SSM_STATE = 64
DILATED_BRANCHES = ((128, 1), (512, 4), (2048, 16))
N_EXPERTS = 64
TOP_K = 8
N_EXPERT_GROUPS = 8
TOPK_GROUPS = 4
ROUTED_SCALE = 2.5
LN_EPS = 1e-5
MASK_VALUE = -1e30

ATT_BLOCK = 256
ATT_LOOKBACK = max(w for w, _ in DILATED_BRANCHES) // ATT_BLOCK
ATT_WINDOWS = (2, 4, 6, ATT_LOOKBACK + 1)
QKV_ROWS = 1024
SSM_CHUNK = 128
SSM_PERM_T = 32
SSM_CH_BLOCK = 128
POST_ROWS = 1024
MOE_ROWS = 1024
MOE_EXPERTS_PER_STEP = 4
VMEM_LIMIT = 56 * 1024 * 1024


def _params(sem, vmem=VMEM_LIMIT):
    return pltpu.CompilerParams(dimension_semantics=sem, vmem_limit_bytes=vmem)


def _qkv_kernel(x_ref, w_ref, pos_ref, q_ref, k_ref, v_ref):
    z = jnp.dot(x_ref[0].astype(BF16), w_ref[...], preferred_element_type=F32).astype(BF16)
    lane = lax.broadcasted_iota(jnp.int32, (z.shape[0], HEAD_LANES), 1)
    for c, o_ref in enumerate((q_ref, k_ref, v_ref)):
        for h in range(N_HEADS):
            zc = z[:, c * ATTN_WIDTH + (h // 2) * HEAD_LANES:c * ATTN_WIDTH + (h // 2 + 1) * HEAD_LANES]
            pos = pos_ref[:, (c * N_HEADS + h) * HEAD_LANES:(c * N_HEADS + h + 1) * HEAD_LANES]
            data = (lane < HEAD_DIM) if h % 2 == 0 else (lane >= HEAD_DIM)
            o_ref[0, h] = jnp.where(data, zc, pos)


def _qkv_tables(seq):
    t = np.arange(seq)
    width = N_HEADS * HEAD_LANES
    qpos = np.zeros((seq, width), np.float32)
    kpos = np.zeros((seq, width), np.float32)
    vpos = np.zeros((seq, width), np.float32)
    for h in range(N_HEADS):
        slope = 2.0 ** (-8.0 * (h + 1) / N_HEADS)
        base = h * HEAD_LANES + (HEAD_DIM if h % 2 == 0 else 0)
        qpos[:, base + 0] = -slope * (t % ATT_BLOCK)
        qpos[:, base + 1] = -slope * ATT_BLOCK * (t // ATT_BLOCK)
        qpos[:, base + 2] = slope
        qpos[:, base + 3] = slope * ATT_BLOCK
        kpos[:, base + 0] = 1.0
        kpos[:, base + 1] = 1.0
        kpos[:, base + 2] = t % ATT_BLOCK
        kpos[:, base + 3] = t // ATT_BLOCK
        ones_lane = h * HEAD_LANES + (HEAD_DIM if h % 2 == 0 else 0)
        vpos[:, ones_lane] = 1.0
    tables = np.concatenate([qpos, kpos, vpos], axis=1)
    assert np.array_equal(tables.astype(BF16).astype(np.float32), tables), "helper columns must be bf16-exact"
    return tables


def _qkv_weights(w_in):
    wq = w_in[:, :ATTN_WIDTH] * (HEAD_DIM ** -0.5)
    return jnp.concatenate([wq, w_in[:, ATTN_WIDTH:3 * ATTN_WIDTH]], axis=1).astype(BF16)


def _qkv_call(x, w_qkv, pos):
    b, s, d = x.shape
    width = N_HEADS * HEAD_LANES
    rows = min(QKV_ROWS, s)
    out = jax.ShapeDtypeStruct((b, N_HEADS, s, HEAD_LANES), BF16)
    blk = pl.BlockSpec((1, N_HEADS, rows, HEAD_LANES), lambda si, bi: (bi, 0, si, 0))
    return pl.pallas_call(
        _qkv_kernel,
        grid=(s // rows, b),
        in_specs=[pl.BlockSpec((1, rows, d), lambda si, bi: (bi, si, 0)),
                  pl.BlockSpec(w_qkv.shape, lambda si, bi: (0, 0)),
                  pl.BlockSpec((rows, 3 * width), lambda si, bi: (si, 0))],
        out_specs=[blk, blk, blk],
        out_shape=[out, out, out],
        compiler_params=_params(("arbitrary", "arbitrary")),
        name="qkv_proj",
    )(x, w_qkv, pos)


def _log_multiplicity_table():
    a = (ATT_LOOKBACK - np.arange(2 * ATT_LOOKBACK + 1))[:, None, None]
    i = np.arange(ATT_BLOCK)[None, :, None]
    j = np.arange(ATT_BLOCK)[None, None, :]
    dist = ATT_BLOCK * a + i - j
    mult = np.zeros(dist.shape, np.float64)
    for window, dil in DILATED_BRANCHES:
        mult += (dist >= 0) & (dist <= window) & (dist % dil == 0)
    return np.where(mult > 0, np.log(np.maximum(mult, 1.0)), MASK_VALUE).astype(np.float32)


def _dot_nt(a, b):
    return lax.dot_general(a, b, (((1,), (1,)), ((), ())), preferred_element_type=F32)


def _attn_kernel(windows, q_ref, k_ref, v_ref, bias_ref, o_ref):
    qi = pl.program_id(1)
    lo = 0
    for idx, n_win in enumerate(windows):
        last = idx == len(windows) - 1
        cond = (qi >= lo) if last else jnp.logical_and(qi >= lo, qi < n_win)
        pl.when(cond)(functools.partial(_attn_window, n_win, q_ref, k_ref, v_ref, bias_ref, o_ref))
        lo = n_win


def _attn_window(n_win, q_ref, k_ref, v_ref, bias_ref, o_ref):
    qi = pl.program_id(1)
    first = jnp.maximum(qi - (n_win - 1), 0)
    start = pl.multiple_of(first * ATT_BLOCK, ATT_BLOCK)
    u0 = ATT_LOOKBACK - (qi - first)
    lane = lax.broadcasted_iota(jnp.int32, (ATT_BLOCK, HEAD_LANES), 1)

    window = pl.ds(start, n_win * ATT_BLOCK)

    def scores(h):
        s = _dot_nt(q_ref[0, h], k_ref[0, h, window, :])
        return s + jnp.concatenate([bias_ref[u0 + c] for c in range(n_win)], axis=1)

    def finish(h, s):
        p = jnp.exp(s - jnp.max(s, axis=1, keepdims=True))
        acc = jnp.dot(p.astype(BF16), v_ref[0, h, window, :], preferred_element_type=F32)
        ones_lane = HEAD_DIM if h % 2 == 0 else 0
        return acc * (1.0 / acc[:, ones_lane:ones_lane + 1])

    s_next = scores(0)
    outs = []
    for h in range(N_HEADS):
        s_cur = s_next
        if h + 1 < N_HEADS:
            s_next = scores(h + 1)
        outs.append(finish(h, s_cur))
        if h % 2 == 1:
            o_ref[0, h // 2] = jnp.where(lane < HEAD_DIM, outs[h - 1], outs[h]).astype(BF16)


def _attn_call(q, k, v, bias):
    b, n_heads, s, lanes = q.shape
    n_win = min(ATT_LOOKBACK + 1, s // ATT_BLOCK)
    windows = tuple(sorted({min(w, n_win) for w in ATT_WINDOWS}))
    once = pl.Buffered(1)
    return pl.pallas_call(
        functools.partial(_attn_kernel, windows),
        grid=(b, s // ATT_BLOCK),
        in_specs=[pl.BlockSpec((1, n_heads, ATT_BLOCK, lanes), lambda bi, qi: (bi, 0, qi, 0)),
                  pl.BlockSpec((1, n_heads, s, lanes), lambda bi, qi: (bi, 0, 0, 0)),
                  pl.BlockSpec((1, n_heads, s, lanes), lambda bi, qi: (bi, 0, 0, 0)),
                  pl.BlockSpec(bias.shape, lambda bi, qi: (0, 0, 0), pipeline_mode=once)],
        out_specs=pl.BlockSpec((1, n_heads // 2, ATT_BLOCK, lanes), lambda bi, qi: (bi, 0, qi, 0)),
        out_shape=jax.ShapeDtypeStruct((b, n_heads // 2, s, lanes), BF16),
        compiler_params=_params(("arbitrary", "arbitrary")),
        name="banded_attention",
    )(q, k, v, bias)


def _gelu_tanh(y):
    return 0.5 * y * (1.0 + jnp.tanh(math.sqrt(2.0 / math.pi) * (y + 0.044715 * (y * y * y))))


def _ssm_kernel(x_ref, wu_ref, perm_ref, perm_t_ref, bmat_ref, cmat_ref, a_re_ref, a_im_ref, dskip_ref,
                wglu_ref, bglu_ref, o_ref, st_re, st_im, u_scr, bu_scr, y_scr):
    n_batch, chunk, _ = x_ref.shape
    n_perm = chunk // SSM_PERM_T
    perm_rows = n_batch * SSM_PERM_T
    n_blocks = bmat_ref.shape[0]
    half = bmat_ref.shape[2] // 2

    @pl.when(pl.program_id(0) == 0)
    def _():
        st_re[...] = jnp.zeros_like(st_re)
        st_im[...] = jnp.zeros_like(st_im)

    for g in range(n_perm):
        xg = x_ref[:, g * SSM_PERM_T:(g + 1) * SSM_PERM_T, :].reshape(perm_rows, x_ref.shape[2])
        ug = jnp.dot(xg.astype(BF16), wu_ref[...], preferred_element_type=F32).astype(BF16)
        u_scr[g * perm_rows:(g + 1) * perm_rows, :] = jnp.dot(
            perm_ref[...], ug, preferred_element_type=F32).astype(BF16)

    def drive(j):
        ch = slice(j * SSM_CH_BLOCK, (j + 1) * SSM_CH_BLOCK)
        bu_scr[j % 2] = jnp.dot(u_scr[:, ch], bmat_ref[j], preferred_element_type=F32)

    drive(0)
    for j in range(n_blocks):
        ch = slice(j * SSM_CH_BLOCK, (j + 1) * SSM_CH_BLOCK)
        if j + 1 < n_blocks:
            drive(j + 1)
        buf = bu_scr.at[j % 2]
        a_re = jnp.broadcast_to(a_re_ref[j], (n_batch, half))
        a_im = jnp.broadcast_to(a_im_ref[j], (n_batch, half))

        def step(t, carry, a_re=a_re, a_im=a_im, buf=buf):
            s_re, s_im = carry
            rows = pl.ds(pl.multiple_of(t * n_batch, n_batch), n_batch)
            n_re = a_re * s_re - a_im * s_im + buf[rows, :half]
            n_im = a_re * s_im + a_im * s_re + buf[rows, half:]
            buf[rows, :half] = n_re
            buf[rows, half:] = n_im
            return n_re, n_im

        s_re, s_im = lax.fori_loop(0, chunk, step, (st_re[j], st_im[j]), unroll=True)
        st_re[j] = s_re
        st_im[j] = s_im
        y_scr[:, ch] = jnp.dot(buf[...].astype(BF16), cmat_ref[j], preferred_element_type=F32)

    y = _gelu_tanh(y_scr[...] + dskip_ref[...] * u_scr[...].astype(F32))
    gate = jnp.dot(y.astype(BF16), wglu_ref[...], preferred_element_type=F32) + bglu_ref[...]
    y = (y * jax.nn.sigmoid(gate)).astype(BF16)
    for g in range(n_perm):
        yg = jnp.dot(perm_t_ref[...], y[g * perm_rows:(g + 1) * perm_rows, :], preferred_element_type=F32)
        o_ref[:, g * SSM_PERM_T:(g + 1) * SSM_PERM_T, :] = yg.reshape(
            n_batch, SSM_PERM_T, yg.shape[1]).astype(BF16)


def _ssm_params(lam_re, lam_im, log_dt, b_re, b_im, c_re, c_im):
    groups = lam_re.shape[0]
    per_blk = SSM_CH_BLOCK // SSM_GROUP_CH
    n_blk = groups // per_blk
    dt = jnp.exp(log_dt.astype(F32))[:, None]
    lam_re = lam_re.astype(F32)
    lam_im = lam_im.astype(F32)
    mag = jnp.exp(lam_re * dt)
    bar_re = mag * jnp.cos(lam_im * dt)
    bar_im = mag * jnp.sin(lam_im * dt)
    den = lam_re * lam_re + lam_im * lam_im
    gain_re = ((bar_re - 1.0) * lam_re + bar_im * lam_im) / den
    gain_im = (bar_im * lam_re - (bar_re - 1.0) * lam_im) / den
    bb_re = gain_re[..., None] * b_re - gain_im[..., None] * b_im
    bb_im = gain_re[..., None] * b_im + gain_im[..., None] * b_re
    eye = jnp.eye(per_blk, dtype=F32)

    def in_blocks(m):
        m = m.reshape(n_blk, per_blk, SSM_STATE, SSM_GROUP_CH)
        return jnp.einsum('jgnc,gh->jgchn', m, eye).reshape(n_blk, SSM_CH_BLOCK, per_blk * SSM_STATE)

    def out_blocks(m):
        m = m.reshape(n_blk, per_blk, SSM_GROUP_CH, SSM_STATE)
        return jnp.einsum('jgcn,gh->jhngc', m, eye).reshape(n_blk, per_blk * SSM_STATE, SSM_CH_BLOCK)

    bmat = jnp.concatenate([in_blocks(bb_re), in_blocks(bb_im)], axis=2).astype(BF16)
    cmat = jnp.concatenate([out_blocks(c_re.astype(F32)), out_blocks(-c_im.astype(F32))], axis=1).astype(BF16)
    a_re = bar_re.reshape(n_blk, 1, per_blk * SSM_STATE)
    a_im = bar_im.reshape(n_blk, 1, per_blk * SSM_STATE)
    return bmat, cmat, a_re, a_im


def _time_major_perm(n_batch):
    rows = n_batch * SSM_PERM_T
    perm = np.zeros((rows, rows), np.float32)
    for bi in range(n_batch):
        for tl in range(SSM_PERM_T):
            perm[tl * n_batch + bi, bi * SSM_PERM_T + tl] = 1.0
    return perm


def _ssm_call(x, w_u, bmat, cmat, a_re, a_im, d_skip, w_glu, b_glu):
    b, s, d = x.shape
    width = w_u.shape[1]
    chunk = min(SSM_CHUNK, s)
    rows = b * chunk
    n_blk, _, n_state2 = bmat.shape
    perm = _time_major_perm(b)
    const = lambda *shape: pl.BlockSpec(shape, lambda i: (0,) * len(shape))
    return pl.pallas_call(
        _ssm_kernel,
        grid=(s // chunk,),
        in_specs=[pl.BlockSpec((b, chunk, d), lambda i: (0, i, 0)),
                  const(d, width), const(*perm.shape), const(*perm.shape),
                  const(*bmat.shape), const(*cmat.shape), const(*a_re.shape), const(*a_im.shape),
                  const(1, width), const(width, width), const(1, width)],
        out_specs=pl.BlockSpec((b, chunk, width), lambda i: (0, i, 0)),
        out_shape=jax.ShapeDtypeStruct((b, s, width), BF16),
        scratch_shapes=[pltpu.VMEM((n_blk, b, n_state2 // 2), F32),
                        pltpu.VMEM((n_blk, b, n_state2 // 2), F32),
                        pltpu.VMEM((rows, width), BF16),
                        pltpu.VMEM((2, rows, n_state2), F32),
                        pltpu.VMEM((rows, width), F32)],
        compiler_params=_params(("arbitrary",)),
        name="s5_mixer",
    )(x, w_u, jnp.asarray(perm, BF16), jnp.asarray(perm.T, BF16), bmat, cmat, a_re, a_im,
      d_skip, w_glu, b_glu)


def _layer_norm(v, g, b):
    mu = jnp.mean(v, axis=-1, keepdims=True)
    c = v - mu
    var = jnp.mean(c * c, axis=-1, keepdims=True)
    return c * lax.rsqrt(var + LN_EPS) * g + b


def _router_gates(scores, sel):
    rows = scores.shape[1]
    per_group = N_EXPERTS // N_EXPERT_GROUPS
    neg_inf = -jnp.inf
    sel3 = sel.reshape(N_EXPERT_GROUPS, per_group, rows)
    sub = lax.broadcasted_iota(jnp.int32, sel3.shape, 1)
    m1 = jnp.max(sel3, axis=1, keepdims=True)
    first = jnp.min(jnp.where(sel3 == m1, sub, per_group), axis=1, keepdims=True)
    m2 = jnp.max(jnp.where(sub == first, neg_inf, sel3), axis=1, keepdims=True)
    grp = jnp.broadcast_to(m1 + m2, sel3.shape)
    kept = []
    for g in range(N_EXPERT_GROUPS):
        beaten = jnp.zeros(grp.shape[1:], F32)
        for o in range(N_EXPERT_GROUPS):
            if o == g:
                continue
            wins = (grp[o] >= grp[g]) if o < g else (grp[o] > grp[g])
            beaten = beaten + jnp.where(wins, 1.0, 0.0)
        kept.append(jnp.where(beaten < TOPK_GROUPS, sel3[g], neg_inf))
    work = jnp.stack(kept, axis=0).reshape(N_EXPERTS, rows)
    eidx = lax.broadcasted_iota(jnp.int32, work.shape, 0)
    w = jnp.zeros(work.shape, F32)
    for _ in range(TOP_K):
        m = jnp.max(work, axis=0, keepdims=True)
        pick = jnp.min(jnp.where(work == m, eidx, N_EXPERTS), axis=0, keepdims=True)
        hit = eidx == pick
        w = jnp.where(hit, scores, w)
        work = jnp.where(hit, neg_inf, work)
    return w / jnp.sum(w, axis=0, keepdims=True) * ROUTED_SCALE


def _post_kernel(alpha, x_ref, attn_ref, ssm_ref, p_ref, wout_ref, g1_ref, b1_ref, wr_hi_ref, wr_lo_ref,
                 rbias_ref, wsg_ref, wsu_ref, wsd_ref, wple_ref, wpg_ref, r_ref, h16_ref, gates_ref):
    n_pairs, _, lanes = attn_ref.shape[1:]
    mix = jnp.dot(ssm_ref[...], wout_ref[n_pairs * lanes:, :], preferred_element_type=F32)
    for pair in range(0, n_pairs, 2):
        a2 = jnp.concatenate([attn_ref[0, pair], attn_ref[0, pair + 1]], axis=1)
        mix = mix + jnp.dot(a2, wout_ref[pair * lanes:(pair + 2) * lanes, :], preferred_element_type=F32)
    h = _layer_norm(alpha * x_ref[...] + mix, g1_ref[...], b1_ref[...])
    h16 = h.astype(BF16)
    h16_ref[...] = h16
    h_lo = (h - h16.astype(F32)).astype(BF16)
    logits = _dot_nt(wr_hi_ref[...], h16) + _dot_nt(wr_hi_ref[...], h_lo) + _dot_nt(wr_lo_ref[...], h16)
    sg = jnp.dot(h16, wsg_ref[...], preferred_element_type=F32)
    su = jnp.dot(h16, wsu_ref[...], preferred_element_type=F32)
    pg = jnp.dot(h16, wpg_ref[...], preferred_element_type=F32)
    pe = jnp.dot(p_ref[...].astype(BF16), wple_ref[...], preferred_element_type=F32)
    scores = jax.nn.sigmoid(logits)
    gates = _router_gates(scores, scores + rbias_ref[...])
    gates = jnp.concatenate([gates, jnp.zeros_like(gates)], axis=0)
    gates_ref[...] = gates.T
    shared = jnp.dot((sg * jax.nn.sigmoid(sg) * su).astype(BF16), wsd_ref[...], preferred_element_type=F32)
    r_ref[...] = alpha * h + shared + pe * jax.nn.sigmoid(pg)


def _post_call(alpha, x2, attn, ssm2, p2, w_out, g1, b1, wr_hi, wr_lo, rbias, wsg, wsu, wsd, wple, wpg):
    t, d = x2.shape
    _, n_pairs, s, lanes = attn.shape
    rows = min(POST_ROWS, s)
    per_seq = s // rows
    row_blk = lambda w: pl.BlockSpec((rows, w), lambda i: (i, 0))
    const = lambda a: pl.BlockSpec(a.shape, lambda i: (0,) * a.ndim)
    attn_blk = pl.BlockSpec((1, n_pairs, rows, lanes), lambda i: (i // per_seq, 0, i % per_seq, 0))
    weights = (w_out, g1, b1, wr_hi, wr_lo, rbias, wsg, wsu, wsd, wple, wpg)
    return pl.pallas_call(
        functools.partial(_post_kernel, alpha),
        grid=(t // rows,),
        in_specs=[row_blk(d), attn_blk, row_blk(ssm2.shape[1]), row_blk(p2.shape[1])]
                 + [const(w) for w in weights],
        out_specs=[row_blk(d), row_blk(d), row_blk(2 * N_EXPERTS)],
        out_shape=[jax.ShapeDtypeStruct((t, d), F32), jax.ShapeDtypeStruct((t, d), BF16),
                   jax.ShapeDtypeStruct((t, 2 * N_EXPERTS), F32)],
        compiler_params=_params(("arbitrary",)),
        name="post_mix_router",
    )(x2, attn, ssm2, p2, *weights)


def _moe_kernel(h_ref, gates_ref, r_ref, wg_ref, wu_ref, wd_ref, g2_ref, b2_ref, o_ref, acc_ref, hid_ref):
    step = pl.program_id(1)
    n_exp, _, hidden = wg_ref.shape

    @pl.when(step == 0)
    def _():
        acc_ref[...] = jnp.zeros_like(acc_ref)

    lanes = gates_ref.shape[1]
    g = pltpu.roll(gates_ref[...], (lanes - n_exp * step) % lanes, axis=1)
    h = h_ref[...]

    def project(j):
        return (jnp.dot(h, wg_ref[j], preferred_element_type=F32),
                jnp.dot(h, wu_ref[j], preferred_element_type=F32))

    z_next = project(0)
    for j in range(n_exp):
        zg, zu = z_next
        if j + 1 < n_exp:
            z_next = project(j + 1)
        hid = zg * jax.nn.sigmoid(zg) * zu * g[:, j:j + 1]
        hid_ref[:, j * hidden:(j + 1) * hidden] = hid.astype(BF16)
    wd = wd_ref[...].reshape(n_exp * hidden, wd_ref.shape[2])
    acc_ref[...] += jnp.dot(hid_ref[...], wd, preferred_element_type=F32)

    @pl.when(step == pl.num_programs(1) - 1)
    def _():
        o_ref[...] = _layer_norm(r_ref[...] + acc_ref[...], g2_ref[...], b2_ref[...])


def _moe_call(h16, gates, r, wg, wu, wd, g2, b2):
    t, d = h16.shape
    rows = min(MOE_ROWS, t)
    n_exp = MOE_EXPERTS_PER_STEP
    hidden = wg.shape[2]
    row_blk = lambda w: pl.BlockSpec((rows, w), lambda i, e: (i, 0))
    return pl.pallas_call(
        _moe_kernel,
        grid=(t // rows, wg.shape[0] // n_exp),
        in_specs=[row_blk(d), row_blk(gates.shape[1]), row_blk(d),
                  pl.BlockSpec((n_exp, d, hidden), lambda i, e: (e, 0, 0)),
                  pl.BlockSpec((n_exp, d, hidden), lambda i, e: (e, 0, 0)),
                  pl.BlockSpec((n_exp, hidden, d), lambda i, e: (e, 0, 0)),
                  pl.BlockSpec((1, d), lambda i, e: (0, 0)),
                  pl.BlockSpec((1, d), lambda i, e: (0, 0))],
        out_specs=row_blk(d),
        out_shape=jax.ShapeDtypeStruct((t, d), F32),
        scratch_shapes=[pltpu.VMEM((rows, d), F32), pltpu.VMEM((rows, n_exp * hidden), BF16)],
        compiler_params=_params(("arbitrary", "arbitrary")),
        name="routed_experts",
    )(h16, gates, r, wg, wu, wd, g2, b2)


def _layer(h, p_i, w_in, lam_re, lam_im, log_dt, b_re, b_im, c_re, c_im, d_skip, w_glu, b_glu, w_out,
           ln1_g, ln1_b, w_router, router_bias, w_gate, w_up, w_down, ws_gate, ws_up, ws_down, w_ple,
           w_ple_gate, ln2_g, ln2_b, alpha):
    b, s, d = h.shape
    t = b * s
    row = lambda v: v.reshape(1, -1).astype(F32)

    q, k, v = _qkv_call(h, _qkv_weights(w_in), jnp.asarray(_qkv_tables(s), BF16))
    attn = _attn_call(q, k, v, jnp.asarray(_log_multiplicity_table()))

    bmat, cmat, a_re, a_im = _ssm_params(lam_re, lam_im, log_dt, b_re, b_im, c_re, c_im)
    ssm = _ssm_call(h, w_in[:, 3 * ATTN_WIDTH:].astype(BF16), bmat, cmat, a_re, a_im,
                    row(d_skip), w_glu.astype(BF16), row(b_glu))

    wr_t = w_router.astype(F32).T
    wr_hi = wr_t.astype(BF16)
    wr_lo = (wr_t - wr_hi.astype(F32)).astype(BF16)
    r, h16, gates = _post_call(
        alpha, h.reshape(t, d), attn, ssm.reshape(t, -1), p_i.reshape(t, -1),
        w_out.astype(BF16), row(ln1_g), row(ln1_b), wr_hi, wr_lo, router_bias.reshape(-1, 1).astype(F32),
        ws_gate.astype(BF16), ws_up.astype(BF16), ws_down.astype(BF16), w_ple.astype(BF16),
        w_ple_gate.astype(BF16))
    out = _moe_call(h16, gates, r, w_gate.astype(BF16), w_up.astype(BF16), w_down.astype(BF16),
                    row(ln2_g), row(ln2_b))
    return out.reshape(b, s, d)


def kernel(x, p, w_in, lam_re, lam_im, log_dt, b_re, b_im, c_re, c_im, d_skip, w_glu, b_glu, w_out, ln1_g, ln1_b, w_router, router_bias, w_gate, w_up, w_down, ws_gate, ws_up, ws_down, w_ple, w_ple_gate, ln2_g, ln2_b):
    depth = w_in.shape[0]
    alpha = (2.0 * depth) ** 0.25
    h = x
    for i in range(depth):
        h = _layer(h, p[i], w_in[i], lam_re[i], lam_im[i], log_dt[i], b_re[i], b_im[i], c_re[i], c_im[i],
                   d_skip[i], w_glu[i], b_glu[i], w_out[i], ln1_g[i], ln1_b[i], w_router[i], router_bias[i],
                   w_gate[i], w_up[i], w_down[i], ws_gate[i], ws_up[i], ws_down[i], w_ple[i], w_ple_gate[i],
                   ln2_g[i], ln2_b[i], alpha)
    return h
```

```python
import functools
import math

import numpy as np
import jax
import jax.numpy as jnp
from jax import lax
from jax.experimental import pallas as pl
from jax.experimental.pallas import tpu as pltpu

F32 = jnp.float32
BF16 = jnp.bfloat16

HEAD_DIM = 64
N_HEADS = 8
ATTN_WIDTH = N_HEADS * HEAD_DIM
HEAD_LANES = 128
SSM_GROUP_CH = 16
SSM_STATE = 64
DILATED_BRANCHES = ((128, 1), (512, 4), (2048, 16))
N_EXPERTS = 64
TOP_K = 8
N_EXPERT_GROUPS = 8
TOPK_GROUPS = 4
ROUTED_SCALE = 2.5
LN_EPS = 1e-5
MASK_VALUE = -1e30

ATT_BLOCK = 256
ATT_LOOKBACK = max(w for w, _ in DILATED_BRANCHES) // ATT_BLOCK
ATT_WINDOWS = tuple(range(2, ATT_LOOKBACK + 2))
QKV_ROWS = 1024
SSM_CHUNK = 128
SSM_PERM_T = 32
SSM_CH_BLOCK = 128
POST_ROWS = 1024
MOE_ROWS = 1024
MOE_EXPERTS_PER_STEP = 4
VMEM_LIMIT = 56 * 1024 * 1024


def _params(sem, vmem=VMEM_LIMIT):
    return pltpu.CompilerParams(dimension_semantics=sem, vmem_limit_bytes=vmem)


def _qkv_kernel(x_ref, w_ref, pos_ref, q_ref, k_ref, v_ref):
    z = jnp.dot(x_ref[0].astype(BF16), w_ref[...], preferred_element_type=F32).astype(BF16)
    lane = lax.broadcasted_iota(jnp.int32, (z.shape[0], HEAD_LANES), 1)
    for c, o_ref in enumerate((q_ref, k_ref, v_ref)):
        for h in range(N_HEADS):
            zc = z[:, c * ATTN_WIDTH + (h // 2) * HEAD_LANES:c * ATTN_WIDTH + (h // 2 + 1) * HEAD_LANES]
            pos = pos_ref[:, (c * N_HEADS + h) * HEAD_LANES:(c * N_HEADS + h + 1) * HEAD_LANES]
            data = (lane < HEAD_DIM) if h % 2 == 0 else (lane >= HEAD_DIM)
            o_ref[0, h] = jnp.where(data, zc, pos)


def _qkv_tables(seq):
    t = np.arange(seq)
    width = N_HEADS * HEAD_LANES
    qpos = np.zeros((seq, width), np.float32)
    kpos = np.zeros((seq, width), np.float32)
    vpos = np.zeros((seq, width), np.float32)
    for h in range(N_HEADS):
        slope = 2.0 ** (-8.0 * (h + 1) / N_HEADS)
        base = h * HEAD_LANES + (HEAD_DIM if h % 2 == 0 else 0)
        qpos[:, base + 0] = -slope * (t % ATT_BLOCK)
        qpos[:, base + 1] = -slope * ATT_BLOCK * (t // ATT_BLOCK)
        qpos[:, base + 2] = slope
        qpos[:, base + 3] = slope * ATT_BLOCK
        kpos[:, base + 0] = 1.0
        kpos[:, base + 1] = 1.0
        kpos[:, base + 2] = t % ATT_BLOCK
        kpos[:, base + 3] = t // ATT_BLOCK
        ones_lane = h * HEAD_LANES + (HEAD_DIM if h % 2 == 0 else 0)
        vpos[:, ones_lane] = 1.0
    tables = np.concatenate([qpos, kpos, vpos], axis=1)
    assert np.array_equal(tables.astype(BF16).astype(np.float32), tables), "helper columns must be bf16-exact"
    return tables


def _qkv_weights(w_in):
    wq = w_in[:, :ATTN_WIDTH] * (HEAD_DIM ** -0.5)
    return jnp.concatenate([wq, w_in[:, ATTN_WIDTH:3 * ATTN_WIDTH]], axis=1).astype(BF16)


def _qkv_call(x, w_qkv, pos):
    b, s, d = x.shape
    width = N_HEADS * HEAD_LANES
    rows = min(QKV_ROWS, s)
    out = jax.ShapeDtypeStruct((b, N_HEADS, s, HEAD_LANES), BF16)
    blk = pl.BlockSpec((1, N_HEADS, rows, HEAD_LANES), lambda si, bi: (bi, 0, si, 0))
    return pl.pallas_call(
        _qkv_kernel,
        grid=(s // rows, b),
        in_specs=[pl.BlockSpec((1, rows, d), lambda si, bi: (bi, si, 0)),
                  pl.BlockSpec(w_qkv.shape, lambda si, bi: (0, 0)),
                  pl.BlockSpec((rows, 3 * width), lambda si, bi: (si, 0))],
        out_specs=[blk, blk, blk],
        out_shape=[out, out, out],
        compiler_params=_params(("arbitrary", "arbitrary")),
        name="qkv_proj",
    )(x, w_qkv, pos)


def _log_multiplicity_table():
    a = (ATT_LOOKBACK - np.arange(2 * ATT_LOOKBACK + 1))[:, None, None]
    i = np.arange(ATT_BLOCK)[None, :, None]
    j = np.arange(ATT_BLOCK)[None, None, :]
    dist = ATT_BLOCK * a + i - j
    mult = np.zeros(dist.shape, np.float64)
    for window, dil in DILATED_BRANCHES:
        mult += (dist >= 0) & (dist <= window) & (dist % dil == 0)
    return np.where(mult > 0, np.log(np.maximum(mult, 1.0)), MASK_VALUE).astype(np.float32)


def _dot_nt(a, b):
    return lax.dot_general(a, b, (((1,), (1,)), ((), ())), preferred_element_type=F32)


def _attn_kernel(windows, q_ref, k_ref, v_ref, bias_ref, o_ref):
    qi = pl.program_id(1)
    lo = 0
    for idx, n_win in enumerate(windows):
        last = idx == len(windows) - 1
        cond = (qi >= lo) if last else jnp.logical_and(qi >= lo, qi < n_win)
        pl.when(cond)(functools.partial(_attn_window, n_win, q_ref, k_ref, v_ref, bias_ref, o_ref))
        lo = n_win


def _attn_window(n_win, q_ref, k_ref, v_ref, bias_ref, o_ref):
    qi = pl.program_id(1)
    first = jnp.maximum(qi - (n_win - 1), 0)
    start = pl.multiple_of(first * ATT_BLOCK, ATT_BLOCK)
    u0 = ATT_LOOKBACK - (qi - first)
    lane = lax.broadcasted_iota(jnp.int32, (ATT_BLOCK, HEAD_LANES), 1)

    window = pl.ds(start, n_win * ATT_BLOCK)

    def scores(h):
        s = _dot_nt(q_ref[0, h], k_ref[0, h, window, :])
        return s + jnp.concatenate([bias_ref[u0 + c] for c in range(n_win)], axis=1)

    def finish(h, s):
        p = jnp.exp(s - jnp.max(s, axis=1, keepdims=True))
        acc = jnp.dot(p.astype(BF16), v_ref[0, h, window, :], preferred_element_type=F32)
        ones_lane = HEAD_DIM if h % 2 == 0 else 0
        return acc * (1.0 / acc[:, ones_lane:ones_lane + 1])

    s_next = scores(0)
    outs = []
    for h in range(N_HEADS):
        s_cur = s_next
        if h + 1 < N_HEADS:
            s_next = scores(h + 1)
        outs.append(finish(h, s_cur))
        if h % 2 == 1:
            o_ref[0, h // 2] = jnp.where(lane < HEAD_DIM, outs[h - 1], outs[h]).astype(BF16)


def _attn_call(q, k, v, bias):
    b, n_heads, s, lanes = q.shape
    n_win = min(ATT_LOOKBACK + 1, s // ATT_BLOCK)
    windows = tuple(sorted({min(w, n_win) for w in ATT_WINDOWS}))
    once = pl.Buffered(1)
    return pl.pallas_call(
        functools.partial(_attn_kernel, windows),
        grid=(b, s // ATT_BLOCK),
        in_specs=[pl.BlockSpec((1, n_heads, ATT_BLOCK, lanes), lambda bi, qi: (bi, 0, qi, 0)),
                  pl.BlockSpec((1, n_heads, s, lanes), lambda bi, qi: (bi, 0, 0, 0)),
                  pl.BlockSpec((1, n_heads, s, lanes), lambda bi, qi: (bi, 0, 0, 0)),
                  pl.BlockSpec(bias.shape, lambda bi, qi: (0, 0, 0), pipeline_mode=once)],
        out_specs=pl.BlockSpec((1, n_heads // 2, ATT_BLOCK, lanes), lambda bi, qi: (bi, 0, qi, 0)),
        out_shape=jax.ShapeDtypeStruct((b, n_heads // 2, s, lanes), BF16),
        compiler_params=_params(("arbitrary", "arbitrary")),
        name="banded_attention",
    )(q, k, v, bias)


def _gelu_tanh(y):
    return 0.5 * y * (1.0 + jnp.tanh(math.sqrt(2.0 / math.pi) * (y + 0.044715 * (y * y * y))))


def _ssm_kernel(x_ref, wu_ref, perm_ref, perm_t_ref, bmat_ref, cmat_ref, a_re_ref, a_im_ref, dskip_ref,
                wglu_ref, bglu_ref, o_ref, st_re, st_im, u_scr, bu_scr, y_scr):
    n_batch, chunk, _ = x_ref.shape
    n_perm = chunk // SSM_PERM_T
    perm_rows = n_batch * SSM_PERM_T
    n_blocks = bmat_ref.shape[0]
    half = bmat_ref.shape[2] // 2

    @pl.when(pl.program_id(0) == 0)
    def _():
        st_re[...] = jnp.zeros_like(st_re)
        st_im[...] = jnp.zeros_like(st_im)

    for g in range(n_perm):
        xg = x_ref[:, g * SSM_PERM_T:(g + 1) * SSM_PERM_T, :].reshape(perm_rows, x_ref.shape[2])
        ug = jnp.dot(xg.astype(BF16), wu_ref[...], preferred_element_type=F32).astype(BF16)
        u_scr[g * perm_rows:(g + 1) * perm_rows, :] = jnp.dot(
            perm_ref[...], ug, preferred_element_type=F32).astype(BF16)

    def drive(j):
        ch = slice(j * SSM_CH_BLOCK, (j + 1) * SSM_CH_BLOCK)
        bu_scr[j % 2] = jnp.dot(u_scr[:, ch], bmat_ref[j], preferred_element_type=F32)

    drive(0)
    for j in range(n_blocks):
        ch = slice(j * SSM_CH_BLOCK, (j + 1) * SSM_CH_BLOCK)
        if j + 1 < n_blocks:
            drive(j + 1)
        buf = bu_scr.at[j % 2]
        a_re = jnp.broadcast_to(a_re_ref[j], (n_batch, half))
        a_im = jnp.broadcast_to(a_im_ref[j], (n_batch, half))

        def step(t, carry, a_re=a_re, a_im=a_im, buf=buf):
            s_re, s_im = carry
            rows = pl.ds(pl.multiple_of(t * n_batch, n_batch), n_batch)
            n_re = a_re * s_re - a_im * s_im + buf[rows, :half]
            n_im = a_re * s_im + a_im * s_re + buf[rows, half:]
            buf[rows, :half] = n_re
            buf[rows, half:] = n_im
            return n_re, n_im

        s_re, s_im = lax.fori_loop(0, chunk, step, (st_re[j], st_im[j]), unroll=True)
        st_re[j] = s_re
        st_im[j] = s_im
        y_scr[:, ch] = jnp.dot(buf[...].astype(BF16), cmat_ref[j], preferred_element_type=F32)

    y = _gelu_tanh(y_scr[...] + dskip_ref[...] * u_scr[...].astype(F32))
    gate = jnp.dot(y.astype(BF16), wglu_ref[...], preferred_element_type=F32) + bglu_ref[...]
    y = (y * jax.nn.sigmoid(gate)).astype(BF16)
    for g in range(n_perm):
        yg = jnp.dot(perm_t_ref[...], y[g * perm_rows:(g + 1) * perm_rows, :], preferred_element_type=F32)
        o_ref[:, g * SSM_PERM_T:(g + 1) * SSM_PERM_T, :] = yg.reshape(
            n_batch, SSM_PERM_T, yg.shape[1]).astype(BF16)


def _ssm_params(lam_re, lam_im, log_dt, b_re, b_im, c_re, c_im):
    groups = lam_re.shape[0]
    per_blk = SSM_CH_BLOCK // SSM_GROUP_CH
    n_blk = groups // per_blk
    dt = jnp.exp(log_dt.astype(F32))[:, None]
    lam_re = lam_re.astype(F32)
    lam_im = lam_im.astype(F32)
    mag = jnp.exp(lam_re * dt)
    bar_re = mag * jnp.cos(lam_im * dt)
    bar_im = mag * jnp.sin(lam_im * dt)
    den = lam_re * lam_re + lam_im * lam_im
    gain_re = ((bar_re - 1.0) * lam_re + bar_im * lam_im) / den
    gain_im = (bar_im * lam_re - (bar_re - 1.0) * lam_im) / den
    bb_re = gain_re[..., None] * b_re - gain_im[..., None] * b_im
    bb_im = gain_re[..., None] * b_im + gain_im[..., None] * b_re
    eye = jnp.eye(per_blk, dtype=F32)

    def in_blocks(m):
        m = m.reshape(n_blk, per_blk, SSM_STATE, SSM_GROUP_CH)
        return jnp.einsum('jgnc,gh->jgchn', m, eye).reshape(n_blk, SSM_CH_BLOCK, per_blk * SSM_STATE)

    def out_blocks(m):
        m = m.reshape(n_blk, per_blk, SSM_GROUP_CH, SSM_STATE)
        return jnp.einsum('jgcn,gh->jhngc', m, eye).reshape(n_blk, per_blk * SSM_STATE, SSM_CH_BLOCK)

    bmat = jnp.concatenate([in_blocks(bb_re), in_blocks(bb_im)], axis=2).astype(BF16)
    cmat = jnp.concatenate([out_blocks(c_re.astype(F32)), out_blocks(-c_im.astype(F32))], axis=1).astype(BF16)
    a_re = bar_re.reshape(n_blk, 1, per_blk * SSM_STATE)
    a_im = bar_im.reshape(n_blk, 1, per_blk * SSM_STATE)
    return bmat, cmat, a_re, a_im


def _time_major_perm(n_batch):
    rows = n_batch * SSM_PERM_T
    perm = np.zeros((rows, rows), np.float32)
    for bi in range(n_batch):
        for tl in range(SSM_PERM_T):
            perm[tl * n_batch + bi, bi * SSM_PERM_T + tl] = 1.0
    return perm


def _ssm_call(x, w_u, bmat, cmat, a_re, a_im, d_skip, w_glu, b_glu):
    b, s, d = x.shape
    width = w_u.shape[1]
    chunk = min(SSM_CHUNK, s)
    rows = b * chunk
    n_blk, _, n_state2 = bmat.shape
    perm = _time_major_perm(b)
    const = lambda *shape: pl.BlockSpec(shape, lambda i: (0,) * len(shape))
    return pl.pallas_call(
        _ssm_kernel,
        grid=(s // chunk,),
        in_specs=[pl.BlockSpec((b, chunk, d), lambda i: (0, i, 0)),
                  const(d, width), const(*perm.shape), const(*perm.shape),
                  const(*bmat.shape), const(*cmat.shape), const(*a_re.shape), const(*a_im.shape),
                  const(1, width), const(width, width), const(1, width)],
        out_specs=pl.BlockSpec((b, chunk, width), lambda i: (0, i, 0)),
        out_shape=jax.ShapeDtypeStruct((b, s, width), BF16),
        scratch_shapes=[pltpu.VMEM((n_blk, b, n_state2 // 2), F32),
                        pltpu.VMEM((n_blk, b, n_state2 // 2), F32),
                        pltpu.VMEM((rows, width), BF16),
                        pltpu.VMEM((2, rows, n_state2), F32),
                        pltpu.VMEM((rows, width), F32)],
        compiler_params=_params(("arbitrary",)),
        name="s5_mixer",
    )(x, w_u, jnp.asarray(perm, BF16), jnp.asarray(perm.T, BF16), bmat, cmat, a_re, a_im,
      d_skip, w_glu, b_glu)


def _layer_norm(v, g, b):
    mu = jnp.mean(v, axis=-1, keepdims=True)
    c = v - mu
    var = jnp.mean(c * c, axis=-1, keepdims=True)
    return c * lax.rsqrt(var + LN_EPS) * g + b


def _router_gates(scores, sel):
    rows = scores.shape[1]
    per_group = N_EXPERTS // N_EXPERT_GROUPS
    neg_inf = -jnp.inf
    sel3 = sel.reshape(N_EXPERT_GROUPS, per_group, rows)
    sub = lax.broadcasted_iota(jnp.int32, sel3.shape, 1)
    m1 = jnp.max(sel3, axis=1, keepdims=True)
    first = jnp.min(jnp.where(sel3 == m1, sub, per_group), axis=1, keepdims=True)
    m2 = jnp.max(jnp.where(sub == first, neg_inf, sel3), axis=1, keepdims=True)
    grp = jnp.broadcast_to(m1 + m2, sel3.shape)
    kept = []
    for g in range(N_EXPERT_GROUPS):
        beaten = jnp.zeros(grp.shape[1:], F32)
        for o in range(N_EXPERT_GROUPS):
            if o == g:
                continue
            wins = (grp[o] >= grp[g]) if o < g else (grp[o] > grp[g])
            beaten = beaten + jnp.where(wins, 1.0, 0.0)
        kept.append(jnp.where(beaten < TOPK_GROUPS, sel3[g], neg_inf))
    work = jnp.stack(kept, axis=0).reshape(N_EXPERTS, rows)
    eidx = lax.broadcasted_iota(jnp.int32, work.shape, 0)
    w = jnp.zeros(work.shape, F32)
    for _ in range(TOP_K):
        m = jnp.max(work, axis=0, keepdims=True)
        pick = jnp.min(jnp.where(work == m, eidx, N_EXPERTS), axis=0, keepdims=True)
        hit = eidx == pick
        w = jnp.where(hit, scores, w)
        work = jnp.where(hit, neg_inf, work)
    return w / jnp.sum(w, axis=0, keepdims=True) * ROUTED_SCALE


def _post_kernel(alpha, x_ref, attn_ref, ssm_ref, p_ref, wout_ref, g1_ref, b1_ref, wr_hi_ref, wr_lo_ref,
                 rbias_ref, wsg_ref, wsu_ref, wsd_ref, wple_ref, wpg_ref, r_ref, h16_ref, gates_ref):
    n_pairs, _, lanes = attn_ref.shape[1:]
    mix = jnp.dot(ssm_ref[...], wout_ref[n_pairs * lanes:, :], preferred_element_type=F32)
    for pair in range(0, n_pairs, 2):
        a2 = jnp.concatenate([attn_ref[0, pair], attn_ref[0, pair + 1]], axis=1)
        mix = mix + jnp.dot(a2, wout_ref[pair * lanes:(pair + 2) * lanes, :], preferred_element_type=F32)
    h = _layer_norm(alpha * x_ref[...] + mix, g1_ref[...], b1_ref[...])
    h16 = h.astype(BF16)
    h16_ref[...] = h16
    h_lo = (h - h16.astype(F32)).astype(BF16)
    logits = _dot_nt(wr_hi_ref[...], h16) + _dot_nt(wr_hi_ref[...], h_lo) + _dot_nt(wr_lo_ref[...], h16)
    sg = jnp.dot(h16, wsg_ref[...], preferred_element_type=F32)
    su = jnp.dot(h16, wsu_ref[...], preferred_element_type=F32)
    pg = jnp.dot(h16, wpg_ref[...], preferred_element_type=F32)
    pe = jnp.dot(p_ref[...].astype(BF16), wple_ref[...], preferred_element_type=F32)
    scores = jax.nn.sigmoid(logits)
    gates = _router_gates(scores, scores + rbias_ref[...])
    gates = jnp.concatenate([gates, jnp.zeros_like(gates)], axis=0)
    gates_ref[...] = gates.T
    shared = jnp.dot((sg * jax.nn.sigmoid(sg) * su).astype(BF16), wsd_ref[...], preferred_element_type=F32)
    r_ref[...] = alpha * h + shared + pe * jax.nn.sigmoid(pg)


def _post_call(alpha, x2, attn, ssm2, p2, w_out, g1, b1, wr_hi, wr_lo, rbias, wsg, wsu, wsd, wple, wpg):
    t, d = x2.shape
    _, n_pairs, s, lanes = attn.shape
    rows = min(POST_ROWS, s)
    per_seq = s // rows
    row_blk = lambda w: pl.BlockSpec((rows, w), lambda i: (i, 0))
    const = lambda a: pl.BlockSpec(a.shape, lambda i: (0,) * a.ndim)
    attn_blk = pl.BlockSpec((1, n_pairs, rows, lanes), lambda i: (i // per_seq, 0, i % per_seq, 0))
    weights = (w_out, g1, b1, wr_hi, wr_lo, rbias, wsg, wsu, wsd, wple, wpg)
    return pl.pallas_call(
        functools.partial(_post_kernel, alpha),
        grid=(t // rows,),
        in_specs=[row_blk(d), attn_blk, row_blk(ssm2.shape[1]), row_blk(p2.shape[1])]
                 + [const(w) for w in weights],
        out_specs=[row_blk(d), row_blk(d), row_blk(2 * N_EXPERTS)],
        out_shape=[jax.ShapeDtypeStruct((t, d), F32), jax.ShapeDtypeStruct((t, d), BF16),
                   jax.ShapeDtypeStruct((t, 2 * N_EXPERTS), F32)],
        compiler_params=_params(("arbitrary",)),
        name="post_mix_router",
    )(x2, attn, ssm2, p2, *weights)


def _moe_kernel(h_ref, gates_ref, r_ref, wg_ref, wu_ref, wd_ref, g2_ref, b2_ref, o_ref, acc_ref, hid_ref):
    step = pl.program_id(1)
    n_exp, _, hidden = wg_ref.shape

    @pl.when(step == 0)
    def _():
        acc_ref[...] = jnp.zeros_like(acc_ref)

    lanes = gates_ref.shape[1]
    g = pltpu.roll(gates_ref[...], (lanes - n_exp * step) % lanes, axis=1)
    h = h_ref[...]

    def project(j):
        return (jnp.dot(h, wg_ref[j], preferred_element_type=F32),
                jnp.dot(h, wu_ref[j], preferred_element_type=F32))

    z_next = project(0)
    for j in range(n_exp):
        zg, zu = z_next
        if j + 1 < n_exp:
            z_next = project(j + 1)
        hid = zg * jax.nn.sigmoid(zg) * zu * g[:, j:j + 1]
        hid_ref[:, j * hidden:(j + 1) * hidden] = hid.astype(BF16)
    wd = wd_ref[...].reshape(n_exp * hidden, wd_ref.shape[2])
    acc_ref[...] += jnp.dot(hid_ref[...], wd, preferred_element_type=F32)

    @pl.when(step == pl.num_programs(1) - 1)
    def _():
        o_ref[...] = _layer_norm(r_ref[...] + acc_ref[...], g2_ref[...], b2_ref[...])


def _moe_call(h16, gates, r, wg, wu, wd, g2, b2):
    t, d = h16.shape
    rows = min(MOE_ROWS, t)
    n_exp = MOE_EXPERTS_PER_STEP
    hidden = wg.shape[2]
    row_blk = lambda w: pl.BlockSpec((rows, w), lambda i, e: (i, 0))
    return pl.pallas_call(
        _moe_kernel,
        grid=(t // rows, wg.shape[0] // n_exp),
        in_specs=[row_blk(d), row_blk(gates.shape[1]), row_blk(d),
                  pl.BlockSpec((n_exp, d, hidden), lambda i, e: (e, 0, 0)),
                  pl.BlockSpec((n_exp, d, hidden), lambda i, e: (e, 0, 0)),
                  pl.BlockSpec((n_exp, hidden, d), lambda i, e: (e, 0, 0)),
                  pl.BlockSpec((1, d), lambda i, e: (0, 0)),
                  pl.BlockSpec((1, d), lambda i, e: (0, 0))],
        out_specs=row_blk(d),
        out_shape=jax.ShapeDtypeStruct((t, d), F32),
        scratch_shapes=[pltpu.VMEM((rows, d), F32), pltpu.VMEM((rows, n_exp * hidden), BF16)],
        compiler_params=_params(("arbitrary", "arbitrary")),
        name="routed_experts",
    )(h16, gates, r, wg, wu, wd, g2, b2)


def _layer(h, p_i, w_in, lam_re, lam_im, log_dt, b_re, b_im, c_re, c_im, d_skip, w_glu, b_glu, w_out,
           ln1_g, ln1_b, w_router, router_bias, w_gate, w_up, w_down, ws_gate, ws_up, ws_down, w_ple,
           w_ple_gate, ln2_g, ln2_b, alpha):
    b, s, d = h.shape
    t = b * s
    row = lambda v: v.reshape(1, -1).astype(F32)

    q, k, v = _qkv_call(h, _qkv_weights(w_in), jnp.asarray(_qkv_tables(s), BF16))
    attn = _attn_call(q, k, v, jnp.asarray(_log_multiplicity_table()))

    bmat, cmat, a_re, a_im = _ssm_params(lam_re, lam_im, log_dt, b_re, b_im, c_re, c_im)
    ssm = _ssm_call(h, w_in[:, 3 * ATTN_WIDTH:].astype(BF16), bmat, cmat, a_re, a_im,
                    row(d_skip), w_glu.astype(BF16), row(b_glu))

    wr_t = w_router.astype(F32).T
    wr_hi = wr_t.astype(BF16)
    wr_lo = (wr_t - wr_hi.astype(F32)).astype(BF16)
    r, h16, gates = _post_call(
        alpha, h.reshape(t, d), attn, ssm.reshape(t, -1), p_i.reshape(t, -1),
        w_out.astype(BF16), row(ln1_g), row(ln1_b), wr_hi, wr_lo, router_bias.reshape(-1, 1).astype(F32),
        ws_gate.astype(BF16), ws_up.astype(BF16), ws_down.astype(BF16), w_ple.astype(BF16),
        w_ple_gate.astype(BF16))
    out = _moe_call(h16, gates, r, w_gate.astype(BF16), w_up.astype(BF16), w_down.astype(BF16),
                    row(ln2_g), row(ln2_b))
    return out.reshape(b, s, d)


def kernel(x, p, w_in, lam_re, lam_im, log_dt, b_re, b_im, c_re, c_im, d_skip, w_glu, b_glu, w_out, ln1_g, ln1_b, w_router, router_bias, w_gate, w_up, w_down, ws_gate, ws_up, ws_down, w_ple, w_ple_gate, ln2_g, ln2_b):
    depth = w_in.shape[0]
    alpha = (2.0 * depth) ** 0.25
    h = x
    for i in range(depth):
        h = _layer(h, p[i], w_in[i], lam_re[i], lam_im[i], log_dt[i], b_re[i], b_im[i], c_re[i], c_im[i],
                   d_skip[i], w_glu[i], b_glu[i], w_out[i], ln1_g[i], ln1_b[i], w_router[i], router_bias[i],
                   w_gate[i], w_up[i], w_down[i], ws_gate[i], ws_up[i], ws_down[i], w_ple[i], w_ple_gate[i],
                   ln2_g[i], ln2_b[i], alpha)
    return h
```

```python
import functools
import math

import numpy as np
import jax
import jax.numpy as jnp
from jax import lax
from jax.experimental import pallas as pl
from jax.experimental.pallas import tpu as pltpu

F32 = jnp.float32
BF16 = jnp.bfloat16

HEAD_DIM = 64
N_HEADS = 8
ATTN_WIDTH = N_HEADS * HEAD_DIM
HEAD_LANES = 128
SSM_GROUP_CH = 16
SSM_STATE = 64
DILATED_BRANCHES = ((128, 1), (512, 4), (2048, 16))
N_EXPERTS = 64
TOP_K = 8
N_EXPERT_GROUPS = 8
TOPK_GROUPS = 4
ROUTED_SCALE = 2.5
LN_EPS = 1e-5
MASK_VALUE = -1e30

ATT_BLOCK = 256
ATT_LOOKBACK = max(w for w, _ in DILATED_BRANCHES) // ATT_BLOCK
ATT_WINDOWS = (2, 4, 6, ATT_LOOKBACK + 1)
QKV_ROWS = 1024
SSM_CHUNK = 128
SSM_PERM_T = 32
SSM_CH_BLOCK = 128
POST_ROWS = 1024
MOE_ROWS = 1024
MOE_EXPERTS_PER_STEP = 8
VMEM_LIMIT = 60 * 1024 * 1024


def _params(sem, vmem=VMEM_LIMIT):
    return pltpu.CompilerParams(dimension_semantics=sem, vmem_limit_bytes=vmem)


def _qkv_kernel(x_ref, w_ref, pos_ref, q_ref, k_ref, v_ref):
    z = jnp.dot(x_ref[0].astype(BF16), w_ref[...], preferred_element_type=F32).astype(BF16)
    lane = lax.broadcasted_iota(jnp.int32, (z.shape[0], HEAD_LANES), 1)
    for c, o_ref in enumerate((q_ref, k_ref, v_ref)):
        for h in range(N_HEADS):
            zc = z[:, c * ATTN_WIDTH + (h // 2) * HEAD_LANES:c * ATTN_WIDTH + (h // 2 + 1) * HEAD_LANES]
            pos = pos_ref[:, (c * N_HEADS + h) * HEAD_LANES:(c * N_HEADS + h + 1) * HEAD_LANES]
            data = (lane < HEAD_DIM) if h % 2 == 0 else (lane >= HEAD_DIM)
            o_ref[0, h] = jnp.where(data, zc, pos)


def _qkv_tables(seq):
    t = np.arange(seq)
    width = N_HEADS * HEAD_LANES
    qpos = np.zeros((seq, width), np.float32)
    kpos = np.zeros((seq, width), np.float32)
    vpos = np.zeros((seq, width), np.float32)
    for h in range(N_HEADS):
        slope = 2.0 ** (-8.0 * (h + 1) / N_HEADS)
        base = h * HEAD_LANES + (HEAD_DIM if h % 2 == 0 else 0)
        qpos[:, base + 0] = -slope * (t % ATT_BLOCK)
        qpos[:, base + 1] = -slope * ATT_BLOCK * (t // ATT_BLOCK)
        qpos[:, base + 2] = slope
        qpos[:, base + 3] = slope * ATT_BLOCK
        kpos[:, base + 0] = 1.0
        kpos[:, base + 1] = 1.0
        kpos[:, base + 2] = t % ATT_BLOCK
        kpos[:, base + 3] = t // ATT_BLOCK
        ones_lane = h * HEAD_LANES + (HEAD_DIM if h % 2 == 0 else 0)
        vpos[:, ones_lane] = 1.0
    tables = np.concatenate([qpos, kpos, vpos], axis=1)
    assert np.array_equal(tables.astype(BF16).astype(np.float32), tables), "helper columns must be bf16-exact"
    return tables


def _qkv_weights(w_in):
    wq = w_in[:, :ATTN_WIDTH] * (HEAD_DIM ** -0.5)
    return jnp.concatenate([wq, w_in[:, ATTN_WIDTH:3 * ATTN_WIDTH]], axis=1).astype(BF16)


def _qkv_call(x, w_qkv, pos):
    b, s, d = x.shape
    width = N_HEADS * HEAD_LANES
    rows = min(QKV_ROWS, s)
    out = jax.ShapeDtypeStruct((b, N_HEADS, s, HEAD_LANES), BF16)
    blk = pl.BlockSpec((1, N_HEADS, rows, HEAD_LANES), lambda si, bi: (bi, 0, si, 0))
    return pl.pallas_call(
        _qkv_kernel,
        grid=(s // rows, b),
        in_specs=[pl.BlockSpec((1, rows, d), lambda si, bi: (bi, si, 0)),
                  pl.BlockSpec(w_qkv.shape, lambda si, bi: (0, 0)),
                  pl.BlockSpec((rows, 3 * width), lambda si, bi: (si, 0))],
        out_specs=[blk, blk, blk],
        out_shape=[out, out, out],
        compiler_params=_params(("arbitrary", "arbitrary")),
        name="qkv_proj",
    )(x, w_qkv, pos)


def _log_multiplicity_table():
    a = (ATT_LOOKBACK - np.arange(2 * ATT_LOOKBACK + 1))[:, None, None]
    i = np.arange(ATT_BLOCK)[None, :, None]
    j = np.arange(ATT_BLOCK)[None, None, :]
    dist = ATT_BLOCK * a + i - j
    mult = np.zeros(dist.shape, np.float64)
    for window, dil in DILATED_BRANCHES:
        mult += (dist >= 0) & (dist <= window) & (dist % dil == 0)
    return np.where(mult > 0, np.log(np.maximum(mult, 1.0)), MASK_VALUE).astype(np.float32)


def _dot_nt(a, b):
    return lax.dot_general(a, b, (((1,), (1,)), ((), ())), preferred_element_type=F32)


def _attn_kernel(windows, q_ref, k_ref, v_ref, bias_ref, o_ref):
    qi = pl.program_id(1)
    lo = 0
    for idx, n_win in enumerate(windows):
        last = idx == len(windows) - 1
        cond = (qi >= lo) if last else jnp.logical_and(qi >= lo, qi < n_win)
        pl.when(cond)(functools.partial(_attn_window, n_win, q_ref, k_ref, v_ref, bias_ref, o_ref))
        lo = n_win


def _attn_window(n_win, q_ref, k_ref, v_ref, bias_ref, o_ref):
    qi = pl.program_id(1)
    first = jnp.maximum(qi - (n_win - 1), 0)
    start = pl.multiple_of(first * ATT_BLOCK, ATT_BLOCK)
    u0 = ATT_LOOKBACK - (qi - first)
    lane = lax.broadcasted_iota(jnp.int32, (ATT_BLOCK, HEAD_LANES), 1)

    window = pl.ds(start, n_win * ATT_BLOCK)

    def scores(h):
        s = _dot_nt(q_ref[0, h], k_ref[0, h, window, :])
        return s + jnp.concatenate([bias_ref[u0 + c] for c in range(n_win)], axis=1)

    def finish(h, s):
        p = jnp.exp(s - jnp.max(s, axis=1, keepdims=True))
        acc = jnp.dot(p.astype(BF16), v_ref[0, h, window, :], preferred_element_type=F32)
        ones_lane = HEAD_DIM if h % 2 == 0 else 0
        return acc * (1.0 / acc[:, ones_lane:ones_lane + 1])

    s_next = scores(0)
    outs = []
    for h in range(N_HEADS):
        s_cur = s_next
        if h + 1 < N_HEADS:
            s_next = scores(h + 1)
        outs.append(finish(h, s_cur))
        if h % 2 == 1:
            o_ref[0, h // 2] = jnp.where(lane < HEAD_DIM, outs[h - 1], outs[h]).astype(BF16)


def _attn_call(q, k, v, bias):
    b, n_heads, s, lanes = q.shape
    n_win = min(ATT_LOOKBACK + 1, s // ATT_BLOCK)
    windows = tuple(sorted({min(w, n_win) for w in ATT_WINDOWS}))
    once = pl.Buffered(1)
    return pl.pallas_call(
        functools.partial(_attn_kernel, windows),
        grid=(b, s // ATT_BLOCK),
        in_specs=[pl.BlockSpec((1, n_heads, ATT_BLOCK, lanes), lambda bi, qi: (bi, 0, qi, 0)),
                  pl.BlockSpec((1, n_heads, s, lanes), lambda bi, qi: (bi, 0, 0, 0)),
                  pl.BlockSpec((1, n_heads, s, lanes), lambda bi, qi: (bi, 0, 0, 0)),
                  pl.BlockSpec(bias.shape, lambda bi, qi: (0, 0, 0), pipeline_mode=once)],
        out_specs=pl.BlockSpec((1, n_heads // 2, ATT_BLOCK, lanes), lambda bi, qi: (bi, 0, qi, 0)),
        out_shape=jax.ShapeDtypeStruct((b, n_heads // 2, s, lanes), BF16),
        compiler_params=_params(("arbitrary", "arbitrary")),
        name="banded_attention",
    )(q, k, v, bias)


def _gelu_tanh(y):
    return 0.5 * y * (1.0 + jnp.tanh(math.sqrt(2.0 / math.pi) * (y + 0.044715 * (y * y * y))))


def _ssm_kernel(x_ref, wu_ref, perm_ref, perm_t_ref, bmat_ref, cmat_ref, a_re_ref, a_im_ref, dskip_ref,
                wglu_ref, bglu_ref, o_ref, st_re, st_im, u_scr, bu_scr, y_scr):
    n_batch, chunk, _ = x_ref.shape
    n_perm = chunk // SSM_PERM_T
    perm_rows = n_batch * SSM_PERM_T
    n_blocks = bmat_ref.shape[0]
    half = bmat_ref.shape[2] // 2

    @pl.when(pl.program_id(0) == 0)
    def _():
        st_re[...] = jnp.zeros_like(st_re)
        st_im[...] = jnp.zeros_like(st_im)

    for g in range(n_perm):
        xg = x_ref[:, g * SSM_PERM_T:(g + 1) * SSM_PERM_T, :].reshape(perm_rows, x_ref.shape[2])
        ug = jnp.dot(xg.astype(BF16), wu_ref[...], preferred_element_type=F32).astype(BF16)
        u_scr[g * perm_rows:(g + 1) * perm_rows, :] = jnp.dot(
            perm_ref[...], ug, preferred_element_type=F32).astype(BF16)

    def drive(j):
        ch = slice(j * SSM_CH_BLOCK, (j + 1) * SSM_CH_BLOCK)
        bu_scr[j % 2] = jnp.dot(u_scr[:, ch], bmat_ref[j], preferred_element_type=F32)

    drive(0)
    for j in range(n_blocks):
        ch = slice(j * SSM_CH_BLOCK, (j + 1) * SSM_CH_BLOCK)
        if j + 1 < n_blocks:
            drive(j + 1)
        buf = bu_scr.at[j % 2]
        a_re = jnp.broadcast_to(a_re_ref[j], (n_batch, half))
        a_im = jnp.broadcast_to(a_im_ref[j], (n_batch, half))

        def step(t, carry, a_re=a_re, a_im=a_im, buf=buf):
            s_re, s_im = carry
            rows = pl.ds(pl.multiple_of(t * n_batch, n_batch), n_batch)
            n_re = a_re * s_re - a_im * s_im + buf[rows, :half]
            n_im = a_re * s_im + a_im * s_re + buf[rows, half:]
            buf[rows, :half] = n_re
            buf[rows, half:] = n_im
            return n_re, n_im

        s_re, s_im = lax.fori_loop(0, chunk, step, (st_re[j], st_im[j]), unroll=True)
        st_re[j] = s_re
        st_im[j] = s_im
        y_scr[:, ch] = jnp.dot(buf[...].astype(BF16), cmat_ref[j], preferred_element_type=F32)

    y = _gelu_tanh(y_scr[...] + dskip_ref[...] * u_scr[...].astype(F32))
    gate = jnp.dot(y.astype(BF16), wglu_ref[...], preferred_element_type=F32) + bglu_ref[...]
    y = (y * jax.nn.sigmoid(gate)).astype(BF16)
    for g in range(n_perm):
        yg = jnp.dot(perm_t_ref[...], y[g * perm_rows:(g + 1) * perm_rows, :], preferred_element_type=F32)
        o_ref[:, g * SSM_PERM_T:(g + 1) * SSM_PERM_T, :] = yg.reshape(
            n_batch, SSM_PERM_T, yg.shape[1]).astype(BF16)


def _ssm_params(lam_re, lam_im, log_dt, b_re, b_im, c_re, c_im):
    groups = lam_re.shape[0]
    per_blk = SSM_CH_BLOCK // SSM_GROUP_CH
    n_blk = groups // per_blk
    dt = jnp.exp(log_dt.astype(F32))[:, None]
    lam_re = lam_re.astype(F32)
    lam_im = lam_im.astype(F32)
    mag = jnp.exp(lam_re * dt)
    bar_re = mag * jnp.cos(lam_im * dt)
    bar_im = mag * jnp.sin(lam_im * dt)
    den = lam_re * lam_re + lam_im * lam_im
    gain_re = ((bar_re - 1.0) * lam_re + bar_im * lam_im) / den
    gain_im = (bar_im * lam_re - (bar_re - 1.0) * lam_im) / den
    bb_re = gain_re[..., None] * b_re - gain_im[..., None] * b_im
    bb_im = gain_re[..., None] * b_im + gain_im[..., None] * b_re
    eye = jnp.eye(per_blk, dtype=F32)

    def in_blocks(m):
        m = m.reshape(n_blk, per_blk, SSM_STATE, SSM_GROUP_CH)
        return jnp.einsum('jgnc,gh->jgchn', m, eye).reshape(n_blk, SSM_CH_BLOCK, per_blk * SSM_STATE)

    def out_blocks(m):
        m = m.reshape(n_blk, per_blk, SSM_GROUP_CH, SSM_STATE)
        return jnp.einsum('jgcn,gh->jhngc', m, eye).reshape(n_blk, per_blk * SSM_STATE, SSM_CH_BLOCK)

    bmat = jnp.concatenate([in_blocks(bb_re), in_blocks(bb_im)], axis=2).astype(BF16)
    cmat = jnp.concatenate([out_blocks(c_re.astype(F32)), out_blocks(-c_im.astype(F32))], axis=1).astype(BF16)
    a_re = bar_re.reshape(n_blk, 1, per_blk * SSM_STATE)
    a_im = bar_im.reshape(n_blk, 1, per_blk * SSM_STATE)
    return bmat, cmat, a_re, a_im


def _time_major_perm(n_batch):
    rows = n_batch * SSM_PERM_T
    perm = np.zeros((rows, rows), np.float32)
    for bi in range(n_batch):
        for tl in range(SSM_PERM_T):
            perm[tl * n_batch + bi, bi * SSM_PERM_T + tl] = 1.0
    return perm


def _ssm_call(x, w_u, bmat, cmat, a_re, a_im, d_skip, w_glu, b_glu):
    b, s, d = x.shape
    width = w_u.shape[1]
    chunk = min(SSM_CHUNK, s)
    rows = b * chunk
    n_blk, _, n_state2 = bmat.shape
    perm = _time_major_perm(b)
    const = lambda *shape: pl.BlockSpec(shape, lambda i: (0,) * len(shape))
    return pl.pallas_call(
        _ssm_kernel,
        grid=(s // chunk,),
        in_specs=[pl.BlockSpec((b, chunk, d), lambda i: (0, i, 0)),
                  const(d, width), const(*perm.shape), const(*perm.shape),
                  const(*bmat.shape), const(*cmat.shape), const(*a_re.shape), const(*a_im.shape),
                  const(1, width), const(width, width), const(1, width)],
        out_specs=pl.BlockSpec((b, chunk, width), lambda i: (0, i, 0)),
        out_shape=jax.ShapeDtypeStruct((b, s, width), BF16),
        scratch_shapes=[pltpu.VMEM((n_blk, b, n_state2 // 2), F32),
                        pltpu.VMEM((n_blk, b, n_state2 // 2), F32),
                        pltpu.VMEM((rows, width), BF16),
                        pltpu.VMEM((2, rows, n_state2), F32),
                        pltpu.VMEM((rows, width), F32)],
        compiler_params=_params(("arbitrary",)),
        name="s5_mixer",
    )(x, w_u, jnp.asarray(perm, BF16), jnp.asarray(perm.T, BF16), bmat, cmat, a_re, a_im,
      d_skip, w_glu, b_glu)


def _layer_norm(v, g, b):
    mu = jnp.mean(v, axis=-1, keepdims=True)
    c = v - mu
    var = jnp.mean(c * c, axis=-1, keepdims=True)
    return c * lax.rsqrt(var + LN_EPS) * g + b


def _router_gates(scores, sel):
    rows = scores.shape[1]
    per_group = N_EXPERTS // N_EXPERT_GROUPS
    neg_inf = -jnp.inf
    sel3 = sel.reshape(N_EXPERT_GROUPS, per_group, rows)
    sub = lax.broadcasted_iota(jnp.int32, sel3.shape, 1)
    m1 = jnp.max(sel3, axis=1, keepdims=True)
    first = jnp.min(jnp.where(sel3 == m1, sub, per_group), axis=1, keepdims=True)
    m2 = jnp.max(jnp.where(sub == first, neg_inf, sel3), axis=1, keepdims=True)
    grp = jnp.broadcast_to(m1 + m2, sel3.shape)
    kept = []
    for g in range(N_EXPERT_GROUPS):
        beaten = jnp.zeros(grp.shape[1:], F32)
        for o in range(N_EXPERT_GROUPS):
            if o == g:
                continue
            wins = (grp[o] >= grp[g]) if o < g else (grp[o] > grp[g])
            beaten = beaten + jnp.where(wins, 1.0, 0.0)
        kept.append(jnp.where(beaten < TOPK_GROUPS, sel3[g], neg_inf))
    work = jnp.stack(kept, axis=0).reshape(N_EXPERTS, rows)
    eidx = lax.broadcasted_iota(jnp.int32, work.shape, 0)
    w = jnp.zeros(work.shape, F32)
    for _ in range(TOP_K):
        m = jnp.max(work, axis=0, keepdims=True)
        pick = jnp.min(jnp.where(work == m, eidx, N_EXPERTS), axis=0, keepdims=True)
        hit = eidx == pick
        w = jnp.where(hit, scores, w)
        work = jnp.where(hit, neg_inf, work)
    return w / jnp.sum(w, axis=0, keepdims=True) * ROUTED_SCALE


def _post_kernel(alpha, x_ref, attn_ref, ssm_ref, p_ref, wout_ref, g1_ref, b1_ref, wr_hi_ref, wr_lo_ref,
                 rbias_ref, wsg_ref, wsu_ref, wsd_ref, wple_ref, wpg_ref, r_ref, h16_ref, gates_ref):
    n_pairs, _, lanes = attn_ref.shape[1:]
    mix = jnp.dot(ssm_ref[...], wout_ref[n_pairs * lanes:, :], preferred_element_type=F32)
    for pair in range(0, n_pairs, 2):
        a2 = jnp.concatenate([attn_ref[0, pair], attn_ref[0, pair + 1]], axis=1)
        mix = mix + jnp.dot(a2, wout_ref[pair * lanes:(pair + 2) * lanes, :], preferred_element_type=F32)
    h = _layer_norm(alpha * x_ref[...] + mix, g1_ref[...], b1_ref[...])
    h16 = h.astype(BF16)
    h16_ref[...] = h16
    h_lo = (h - h16.astype(F32)).astype(BF16)
    logits = _dot_nt(wr_hi_ref[...], h16) + _dot_nt(wr_hi_ref[...], h_lo) + _dot_nt(wr_lo_ref[...], h16)
    sg = jnp.dot(h16, wsg_ref[...], preferred_element_type=F32)
    su = jnp.dot(h16, wsu_ref[...], preferred_element_type=F32)
    pg = jnp.dot(h16, wpg_ref[...], preferred_element_type=F32)
    pe = jnp.dot(p_ref[...].astype(BF16), wple_ref[...], preferred_element_type=F32)
    scores = jax.nn.sigmoid(logits)
    gates = _router_gates(scores, scores + rbias_ref[...])
    gates = jnp.concatenate([gates, jnp.zeros_like(gates)], axis=0)
    gates_ref[...] = gates.T
    shared = jnp.dot((sg * jax.nn.sigmoid(sg) * su).astype(BF16), wsd_ref[...], preferred_element_type=F32)
    r_ref[...] = alpha * h + shared + pe * jax.nn.sigmoid(pg)


def _post_call(alpha, x2, attn, ssm2, p2, w_out, g1, b1, wr_hi, wr_lo, rbias, wsg, wsu, wsd, wple, wpg):
    t, d = x2.shape
    _, n_pairs, s, lanes = attn.shape
    rows = min(POST_ROWS, s)
    per_seq = s // rows
    row_blk = lambda w: pl.BlockSpec((rows, w), lambda i: (i, 0))
    const = lambda a: pl.BlockSpec(a.shape, lambda i: (0,) * a.ndim)
    attn_blk = pl.BlockSpec((1, n_pairs, rows, lanes), lambda i: (i // per_seq, 0, i % per_seq, 0))
    weights = (w_out, g1, b1, wr_hi, wr_lo, rbias, wsg, wsu, wsd, wple, wpg)
    return pl.pallas_call(
        functools.partial(_post_kernel, alpha),
        grid=(t // rows,),
        in_specs=[row_blk(d), attn_blk, row_blk(ssm2.shape[1]), row_blk(p2.shape[1])]
                 + [const(w) for w in weights],
        out_specs=[row_blk(d), row_blk(d), row_blk(2 * N_EXPERTS)],
        out_shape=[jax.ShapeDtypeStruct((t, d), F32), jax.ShapeDtypeStruct((t, d), BF16),
                   jax.ShapeDtypeStruct((t, 2 * N_EXPERTS), F32)],
        compiler_params=_params(("arbitrary",)),
        name="post_mix_router",
    )(x2, attn, ssm2, p2, *weights)


def _moe_kernel(h_ref, gates_ref, r_ref, wg_ref, wu_ref, wd_ref, g2_ref, b2_ref, o_ref, hid_ref):
    step = pl.program_id(1)
    n_exp, _, hidden = wg_ref.shape

    @pl.when(step == 0)
    def _():
        o_ref[...] = jnp.zeros_like(o_ref)

    lanes = gates_ref.shape[1]
    g = pltpu.roll(gates_ref[...], (lanes - n_exp * step) % lanes, axis=1)
    h = h_ref[...]

    def project(j):
        return (jnp.dot(h, wg_ref[j], preferred_element_type=F32),
                jnp.dot(h, wu_ref[j], preferred_element_type=F32))

    z_next = project(0)
    for j in range(n_exp):
        zg, zu = z_next
        if j + 1 < n_exp:
            z_next = project(j + 1)
        hid = zg * jax.nn.sigmoid(zg) * zu * g[:, j:j + 1]
        hid_ref[:, j * hidden:(j + 1) * hidden] = hid.astype(BF16)
    wd = wd_ref[...].reshape(n_exp * hidden, wd_ref.shape[2])
    o_ref[...] += jnp.dot(hid_ref[...], wd, preferred_element_type=F32)

    @pl.when(step == pl.num_programs(1) - 1)
    def _():
        o_ref[...] = _layer_norm(r_ref[...] + o_ref[...], g2_ref[...], b2_ref[...])


def _moe_call(h16, gates, r, wg, wu, wd, g2, b2):
    t, d = h16.shape
    rows = min(MOE_ROWS, t)
    n_exp = MOE_EXPERTS_PER_STEP
    hidden = wg.shape[2]
    row_blk = lambda w: pl.BlockSpec((rows, w), lambda i, e: (i, 0))
    return pl.pallas_call(
        _moe_kernel,
        grid=(t // rows, wg.shape[0] // n_exp),
        in_specs=[row_blk(d), row_blk(gates.shape[1]), row_blk(d),
                  pl.BlockSpec((n_exp, d, hidden), lambda i, e: (e, 0, 0)),
                  pl.BlockSpec((n_exp, d, hidden), lambda i, e: (e, 0, 0)),
                  pl.BlockSpec((n_exp, hidden, d), lambda i, e: (e, 0, 0)),
                  pl.BlockSpec((1, d), lambda i, e: (0, 0)),
                  pl.BlockSpec((1, d), lambda i, e: (0, 0))],
        out_specs=row_blk(d),
        out_shape=jax.ShapeDtypeStruct((t, d), F32),
        scratch_shapes=[pltpu.VMEM((rows, n_exp * hidden), BF16)],
        compiler_params=_params(("arbitrary", "arbitrary")),
        name="routed_experts",
    )(h16, gates, r, wg, wu, wd, g2, b2)


def _layer(h, p_i, w_in, lam_re, lam_im, log_dt, b_re, b_im, c_re, c_im, d_skip, w_glu, b_glu, w_out,
           ln1_g, ln1_b, w_router, router_bias, w_gate, w_up, w_down, ws_gate, ws_up, ws_down, w_ple,
           w_ple_gate, ln2_g, ln2_b, alpha):
    b, s, d = h.shape
    t = b * s
    row = lambda v: v.reshape(1, -1).astype(F32)

    q, k, v = _qkv_call(h, _qkv_weights(w_in), jnp.asarray(_qkv_tables(s), BF16))
    attn = _attn_call(q, k, v, jnp.asarray(_log_multiplicity_table()))

    bmat, cmat, a_re, a_im = _ssm_params(lam_re, lam_im, log_dt, b_re, b_im, c_re, c_im)
    ssm = _ssm_call(h, w_in[:, 3 * ATTN_WIDTH:].astype(BF16), bmat, cmat, a_re, a_im,
                    row(d_skip), w_glu.astype(BF16), row(b_glu))

    wr_t = w_router.astype(F32).T
    wr_hi = wr_t.astype(BF16)
    wr_lo = (wr_t - wr_hi.astype(F32)).astype(BF16)
    r, h16, gates = _post_call(
        alpha, h.reshape(t, d), attn, ssm.reshape(t, -1), p_i.reshape(t, -1),
        w_out.astype(BF16), row(ln1_g), row(ln1_b), wr_hi, wr_lo, router_bias.reshape(-1, 1).astype(F32),
        ws_gate.astype(BF16), ws_up.astype(BF16), ws_down.astype(BF16), w_ple.astype(BF16),
        w_ple_gate.astype(BF16))
    out = _moe_call(h16, gates, r, w_gate.astype(BF16), w_up.astype(BF16), w_down.astype(BF16),
                    row(ln2_g), row(ln2_b))
    return out.reshape(b, s, d)


def kernel(x, p, w_in, lam_re, lam_im, log_dt, b_re, b_im, c_re, c_im, d_skip, w_glu, b_glu, w_out, ln1_g, ln1_b, w_router, router_bias, w_gate, w_up, w_down, ws_gate, ws_up, ws_down, w_ple, w_ple_gate, ln2_g, ln2_b):
    depth = w_in.shape[0]
    alpha = (2.0 * depth) ** 0.25
    h = x
    for i in range(depth):
        h = _layer(h, p[i], w_in[i], lam_re[i], lam_im[i], log_dt[i], b_re[i], b_im[i], c_re[i], c_im[i],
                   d_skip[i], w_glu[i], b_glu[i], w_out[i], ln1_g[i], ln1_b[i], w_router[i], router_bias[i],
                   w_gate[i], w_up[i], w_down[i], ws_gate[i], ws_up[i], ws_down[i], w_ple[i], w_ple_gate[i],
                   ln2_g[i], ln2_b[i], alpha)
    return h
```
